```python
import jax, jax.numpy as jnp
from jax import lax
import numpy as np

D_MODEL = 1024
BATCH = 2
SEQ = 8192
DEPTH = 1

ATTN_WIDTH = D_MODEL // 2
ATTN_HEADS = 8
ATTN_HEAD_DIM = ATTN_WIDTH // ATTN_HEADS
DILATED_PAIRS = ((128, 1), (512, 4), (2048, 16))
ATTN_BLOCK = 128
ROPE_THETA = 10000.0
HGRN_WIDTH = D_MODEL - ATTN_WIDTH
HGRN_EXPAND = 128
HGRN_HEADS = HGRN_WIDTH // HGRN_EXPAND
HGRN_CHUNK = 16
MIX_WIDTH = ATTN_WIDTH + HGRN_WIDTH
IN_PROJ_WIDTH = 3 * ATTN_WIDTH + 4 * HGRN_WIDTH
FFN_HIDDEN = ((-(-8 * D_MODEL // 3) + 255) // 256) * 256
NORM_EPS = 1e-6

kernel_name = "hymba_dilated_attn_hgrn2_block"


def rmsnorm(x, w):
    xf = x.astype(jnp.float32)
    y = xf * lax.rsqrt(jnp.mean(xf * xf, axis=-1, keepdims=True) + NORM_EPS)
    return (y * w.astype(jnp.float32)).astype(x.dtype)


def rotary(x):
    S, Dh = x.shape[1], x.shape[3]
    half = Dh // 2
    inv_freq = ROPE_THETA ** (-jnp.arange(half, dtype=jnp.float32) / half)
    ang = jnp.arange(S, dtype=jnp.float32)[:, None] * inv_freq[None, :]
    cos = jnp.cos(ang)[None, :, None, :]
    sin = jnp.sin(ang)[None, :, None, :]
    xf = x.astype(jnp.float32)
    x1, x2 = xf[..., :half], xf[..., half:]
    return jnp.concatenate([x1 * cos - x2 * sin, x2 * cos + x1 * sin], axis=-1)


def dilated_window_attention(q, k, v, window, dilation):
    B, S, H, Dh = q.shape
    L = S // dilation
    W = window // dilation
    n_blk = -(-L // ATTN_BLOCK)
    Lp = n_blk * ATTN_BLOCK

    def to_blocks(t):
        t = t.reshape(B, L, dilation, H, Dh).transpose(0, 2, 1, 3, 4)
        t = jnp.pad(t, ((0, 0), (0, 0), (0, Lp - L), (0, 0), (0, 0)))
        return t.reshape(B, dilation, n_blk, ATTN_BLOCK, H, Dh)

    def with_prev(t):
        prev = jnp.pad(t, ((0, 0), (0, 0), (1, 0), (0, 0), (0, 0), (0, 0)))[:, :, :-1]
        return jnp.concatenate([prev, t], axis=3)

    qb = to_blocks(q)
    kc = with_prev(to_blocks(k))
    vc = with_prev(to_blocks(v))
    scores = jnp.einsum('bdnqhe,bdnkhe->bdnhqk', qb, kc) * (Dh ** -0.5)
    qi = jnp.arange(ATTN_BLOCK)[:, None]
    kj = jnp.arange(2 * ATTN_BLOCK)[None, :]
    delta = ATTN_BLOCK + qi - kj
    blk = jnp.arange(n_blk)[:, None, None]
    valid = (delta >= 0) & (delta <= W) & ((blk > 0) | (kj >= ATTN_BLOCK))[...]
    scores = jnp.where(valid[None, None, :, None], scores, -jnp.inf)
    m = jnp.max(scores, axis=-1, keepdims=True)
    p = jnp.exp(scores - m)
    s = jnp.sum(p, axis=-1, keepdims=True)
    out = jnp.einsum('bdnhqk,bdnkhe->bdnqhe', p, vc) / s.transpose(0, 1, 2, 4, 3, 5)
    lse = (m + jnp.log(s))[..., 0].transpose(0, 1, 2, 4, 3)
    out = out.reshape(B, dilation, Lp, H, Dh)[:, :, :L].transpose(0, 2, 1, 3, 4).reshape(B, S, H, Dh)
    lse = lse.reshape(B, dilation, Lp, H)[:, :, :L].transpose(0, 2, 1, 3).reshape(B, S, H)
    return out, lse


def dilated_attention_group(q, k, v):
    B, S, _ = q.shape
    qh = rotary(q.reshape(B, S, ATTN_HEADS, ATTN_HEAD_DIM))
    kh = rotary(k.reshape(B, S, ATTN_HEADS, ATTN_HEAD_DIM))
    vh = v.reshape(B, S, ATTN_HEADS, ATTN_HEAD_DIM).astype(jnp.float32)
    outs, lses = [], []
    for window, dilation in DILATED_PAIRS:
        o, l = dilated_window_attention(qh, kh, vh, window, dilation)
        outs.append(o)
        lses.append(l)
    weights = jax.nn.softmax(jnp.stack(lses, axis=0), axis=0)
    y = jnp.sum(weights[..., None] * jnp.stack(outs, axis=0), axis=0)
    return y.reshape(B, S, ATTN_WIDTH)


def hgrn2_group(q, f_logit, i, g, lb, norm_w):
    B, S, _ = q.shape
    H, Dk, C = HGRN_HEADS, HGRN_EXPAND, HGRN_CHUNK
    N = S // C
    f = lb + (1.0 - lb) * jax.nn.sigmoid(f_logit.astype(jnp.float32))
    log_f = jnp.log(f)
    key = 1.0 - f
    qf = jax.nn.silu(q.astype(jnp.float32))

    def chunks(t):
        return t.reshape(B, N, C, H, Dk).transpose(0, 3, 1, 2, 4)

    qc, kc, vc, lfc = chunks(qf), chunks(key), chunks(i.astype(jnp.float32)), chunks(log_f)
    b = jnp.cumsum(lfc, axis=3)
    causal = jnp.tril(jnp.ones((C, C), dtype=bool))
    diff = b[:, :, :, :, None, :] - b[:, :, :, None, :, :]
    decay = jnp.exp(jnp.where(causal[:, :, None], diff, -jnp.inf))
    scores = jnp.einsum('bhntd,bhnsd,bhntsd->bhnts', qc, kc, decay)
    o_intra = jnp.einsum('bhnts,bhnsv->bhntv', scores, vc)

    b_last = b[:, :, :, -1:, :]
    q_inter = qc * jnp.exp(b)
    k_upd = kc * jnp.exp(b_last - b)
    chunk_decay = jnp.exp(b_last[:, :, :, 0, :])

    def step(state, xs):
        qn, kn, vn, dn = xs
        o = jnp.einsum('bhtd,bhdv->bhtv', qn, state)
        state = dn[..., None] * state + jnp.einsum('bhtd,bhtv->bhdv', kn, vn)
        return state, o

    xs = (jnp.moveaxis(q_inter, 2, 0), jnp.moveaxis(k_upd, 2, 0),
          jnp.moveaxis(vc, 2, 0), jnp.moveaxis(chunk_decay, 2, 0))
    state0 = jnp.zeros((B, H, Dk, Dk), dtype=jnp.float32)
    _, o_inter = lax.scan(step, state0, xs)
    o = o_intra + jnp.moveaxis(o_inter, 0, 2)
    o = o.transpose(0, 2, 3, 1, 4).reshape(B, S, H, Dk)
    o = o * lax.rsqrt(jnp.mean(o * o, axis=-1, keepdims=True) + NORM_EPS)
    o = o.reshape(B, S, HGRN_WIDTH) * norm_w.astype(jnp.float32)
    return o * jax.nn.silu(g.astype(jnp.float32))


def setup_inputs(seed: int = 0) -> dict:
    key = jax.random.key(seed)
    ks = jax.random.split(key, 10)
    f32 = jnp.float32
    x = jax.random.normal(ks[0], (BATCH, SEQ, D_MODEL), f32)
    norm1_w = 1.0 + 0.02 * jax.random.normal(ks[1], (DEPTH, D_MODEL), f32)
    w_in = jax.random.normal(ks[2], (DEPTH, D_MODEL, IN_PROJ_WIDTH), f32) * D_MODEL ** -0.5
    lb_logits = 0.5 * jax.random.normal(ks[3], (DEPTH + 1, HGRN_WIDTH), f32)
    hgrn_norm_w = 1.0 + 0.02 * jax.random.normal(ks[4], (DEPTH, HGRN_WIDTH), f32)
    w_out = jax.random.normal(ks[5], (DEPTH, MIX_WIDTH, D_MODEL), f32) * MIX_WIDTH ** -0.5
    norm2_w = 1.0 + 0.02 * jax.random.normal(ks[6], (DEPTH, D_MODEL), f32)
    w_gate_up = jax.random.normal(ks[7], (DEPTH, D_MODEL, 2 * FFN_HIDDEN), f32) * D_MODEL ** -0.5
    w_down = jax.random.normal(ks[8], (DEPTH, FFN_HIDDEN, D_MODEL), f32) * FFN_HIDDEN ** -0.5
    final_norm_w = 1.0 + 0.02 * jax.random.normal(ks[9], (D_MODEL,), f32)
    return {"x": x, "norm1_w": norm1_w, "w_in": w_in, "lb_logits": lb_logits,
            "hgrn_norm_w": hgrn_norm_w, "w_out": w_out, "norm2_w": norm2_w,
            "w_gate_up": w_gate_up, "w_down": w_down, "final_norm_w": final_norm_w}


def reference(x, norm1_w, w_in, lb_logits, hgrn_norm_w, w_out, norm2_w, w_gate_up, w_down, final_norm_w):
    lb_table = jnp.cumsum(jax.nn.softmax(lb_logits.astype(jnp.float32), axis=0), axis=0)
    h = x
    for l in range(DEPTH):
        u = rmsnorm(h, norm1_w[l])
        proj = jnp.einsum('bsd,de->bse', u, w_in[l])
        a = ATTN_WIDTH
        qa, ka, va = proj[..., :a], proj[..., a:2 * a], proj[..., 2 * a:3 * a]
        o = 3 * a
        w = HGRN_WIDTH
        qb, fb, ib, gb = (proj[..., o:o + w], proj[..., o + w:o + 2 * w],
                          proj[..., o + 2 * w:o + 3 * w], proj[..., o + 3 * w:o + 4 * w])
        ya = dilated_attention_group(qa, ka, va)
        yb = hgrn2_group(qb, fb, ib, gb, lb_table[l], hgrn_norm_w[l])
        mixed = jnp.concatenate([ya, yb], axis=-1).astype(h.dtype)
        h = h + jnp.einsum('bse,ed->bsd', mixed, w_out[l])
        u2 = rmsnorm(h, norm2_w[l])
        gu = jnp.einsum('bsd,df->bsf', u2, w_gate_up[l])
        gate, up = gu[..., :FFN_HIDDEN], gu[..., FFN_HIDDEN:]
        h = h + jnp.einsum('bsf,fd->bsd', jax.nn.silu(gate) * up, w_down[l])
    return rmsnorm(h, final_norm_w)
```

```python
import functools

import numpy as np
import jax
import jax.numpy as jnp
from jax import lax
from jax.experimental import pallas as pl
from jax.experimental.pallas import tpu as pltpu

D_MODEL = 1024
ATTN_WIDTH = 512
ATTN_HEADS = 8
HEAD_DIM = 64
DILATED_PAIRS = ((128, 1), (512, 4), (2048, 16))
ATTN_BLOCK = 128
ROPE_THETA = 10000.0
HGRN_WIDTH = 512
HGRN_DK = 128
HGRN_HEADS = 4
FFN_HIDDEN = 2816
IN_PROJ_WIDTH = 3 * ATTN_WIDTH + 4 * HGRN_WIDTH
NORM_EPS = 1e-6

LANES = 128
HGRN_BLOCK = 128
HGRN_SUB = 16
VMEM_LIMIT = 56 * 1024 * 1024

F32 = jnp.float32
BF16 = jnp.bfloat16


def _dot(a, b):
    return jnp.dot(a, b, preferred_element_type=F32)


def _dot_nt(a, b):
    return lax.dot_general(a, b, (((1,), (1,)), ((), ())), preferred_element_type=F32)


def _dot_tn(a, b):
    return lax.dot_general(a, b, (((0,), (0,)), ((), ())), preferred_element_type=F32)


def _rms_scale(x):
    return x * lax.rsqrt(jnp.mean(x * x, axis=-1, keepdims=True) + NORM_EPS)


def _sigmoid(x):
    return 1.0 / (1.0 + jnp.exp(-x))


def _resident(shape):
    return pl.BlockSpec(shape, lambda *_: (0,) * len(shape), pipeline_mode=pl.Buffered(1))


def _rope_tables(seq):
    half = HEAD_DIM // 2
    inv_freq = ROPE_THETA ** (-np.arange(half, dtype=np.float64) / half)
    ang = np.arange(seq, dtype=np.float64)[:, None] * inv_freq[None, :]
    cos, sin = np.cos(ang), np.sin(ang)
    cos_t = np.concatenate([cos, cos, cos, cos], axis=1)
    sin_t = np.concatenate([-sin, sin, -sin, sin], axis=1)
    return jnp.asarray(cos_t, F32), jnp.asarray(sin_t, F32)


def _rotate_half_pairs(t, cos, sin):
    lane = lax.broadcasted_iota(jnp.int32, t.shape, 1)
    fwd = pltpu.roll(t, LANES - HEAD_DIM // 2, 1)
    bwd = pltpu.roll(t, HEAD_DIM // 2, 1)
    partner = jnp.where((lane & (HEAD_DIM // 2)) == 0, fwd, bwd)
    return t * cos + partner * sin


def _in_proj_kernel(x_ref, nw_ref, w_ref, cos_ref, sin_ref, q_ref, k_ref, v_ref, hg_ref):
    u = (_rms_scale(x_ref[...]) * nw_ref[...]).astype(BF16)
    cos = cos_ref[...]
    sin = sin_ref[...]
    a = ATTN_WIDTH
    pq = _dot(u, w_ref[:, 0:a])
    pk = _dot(u, w_ref[:, a:2 * a])
    scale = HEAD_DIM ** -0.5
    for j in range(a // LANES):
        sl = slice(j * LANES, (j + 1) * LANES)
        q_ref[:, sl] = (_rotate_half_pairs(pq[:, sl], cos, sin) * scale).astype(BF16)
        k_ref[:, sl] = _rotate_half_pairs(pk[:, sl], cos, sin).astype(BF16)
    v_ref[...] = _dot(u, w_ref[:, 2 * a:3 * a]).astype(BF16)
    hg_ref[...] = _dot(u, w_ref[:, 3 * a:])


def _in_proj(x2, norm_w, w_in_bf, cos_t, sin_t, seq, tm=512):
    rows = x2.shape[0]
    tiles_per_seq = seq // tm
    row_spec = lambda width: pl.BlockSpec((tm, width), lambda i: (i, 0))
    tab_spec = pl.BlockSpec((tm, LANES), lambda i: (i % tiles_per_seq, 0))
    return pl.pallas_call(
        _in_proj_kernel,
        grid=(rows // tm,),
        in_specs=[row_spec(D_MODEL), _resident((1, D_MODEL)),
                  _resident((D_MODEL, IN_PROJ_WIDTH)), tab_spec, tab_spec],
        out_specs=[row_spec(ATTN_WIDTH)] * 3 + [row_spec(4 * HGRN_WIDTH)],
        out_shape=[jax.ShapeDtypeStruct((rows, ATTN_WIDTH), BF16)] * 3
        + [jax.ShapeDtypeStruct((rows, 4 * HGRN_WIDTH), F32)],
        compiler_params=pltpu.CompilerParams(
            dimension_semantics=("arbitrary",), vmem_limit_bytes=VMEM_LIMIT),
        name="in_proj",
    )(x2, norm_w, w_in_bf, cos_t, sin_t)


def _attn_kernel(q_ref, kp_ref, kc_ref, vp_ref, vc_ref, o_ref, l_ref, *, n_qblk):
    n = pl.program_id(2)
    blk = ATTN_BLOCK
    row = lax.broadcasted_iota(jnp.int32, (blk, blk), 0)
    col = lax.broadcasted_iota(jnp.int32, (blk, blk), 1)
    cur_ok = col <= row
    prev_ok = col >= row
    first_ok = jnp.logical_and(prev_ok, n > 0)
    head_a = col < HEAD_DIM
    neg = -jnp.inf
    for qb in range(n_qblk):
        rows = slice(qb * blk, (qb + 1) * blk)
        prows = slice((qb - 1) * blk, qb * blk)
        for hp in range(ATTN_WIDTH // LANES):
            sl = slice(hp * LANES, (hp + 1) * LANES)
            q2 = q_ref[0, rows, sl]
            kc, vc = kc_ref[0, rows, sl], vc_ref[0, rows, sl]
            if qb == 0:
                kp, vp, p_ok = kp_ref[0, :, sl], vp_ref[0, :, sl], first_ok
            else:
                kp, vp, p_ok = kc_ref[0, prows, sl], vc_ref[0, prows, sl], prev_ok
            outs, lses = [], []
            for sel in (head_a, jnp.logical_not(head_a)):
                qh = jnp.where(sel, q2, jnp.zeros_like(q2))
                sc = jnp.where(cur_ok, _dot_nt(qh, kc), neg)
                sp = jnp.where(p_ok, _dot_nt(qh, kp), neg)
                m = jnp.maximum(jnp.max(sc, axis=-1, keepdims=True),
                                jnp.max(sp, axis=-1, keepdims=True))
                pc = jnp.exp(sc - m)
                pp = jnp.exp(sp - m)
                den = jnp.sum(pc, axis=-1, keepdims=True) + jnp.sum(pp, axis=-1, keepdims=True)
                acc = _dot(pc.astype(BF16), vc) + _dot(pp.astype(BF16), vp)
                outs.append(acc / den)
                lses.append(m + jnp.log(den))
            o_ref[0, rows, sl] = jnp.where(head_a, outs[0], outs[1])
            l_ref[0, rows, sl] = jnp.where(head_a, lses[0], lses[1])


def _dilated_attention(q, k, v, batch, seq, dilation, tq=512):
    sub_len = seq // dilation
    tq = min(tq, sub_len)
    n_qblk = tq // ATTN_BLOCK
    shape = (batch, sub_len, dilation * ATTN_WIDTH)
    qv, kv, vv = (t.reshape(shape) for t in (q, k, v))
    cur = pl.BlockSpec((1, tq, ATTN_WIDTH), lambda b, r, n: (b, n, r))
    prev = pl.BlockSpec((1, ATTN_BLOCK, ATTN_WIDTH),
                        lambda b, r, n: (b, jnp.maximum(n * n_qblk - 1, 0), r))
    out, lse = pl.pallas_call(
        functools.partial(_attn_kernel, n_qblk=n_qblk),
        grid=(batch, dilation, sub_len // tq),
        in_specs=[cur, prev, cur, prev, cur],
        out_specs=[cur, cur],
        out_shape=[jax.ShapeDtypeStruct(shape, F32)] * 2,
        compiler_params=pltpu.CompilerParams(
            dimension_semantics=("arbitrary",) * 3, vmem_limit_bytes=VMEM_LIMIT),
        name=f"dilated_attn_d{dilation}",
    )(qv, kv, kv, vv, vv)
    flat = (batch * seq, ATTN_WIDTH)
    return out.reshape(flat), lse.reshape(flat)


def _rows_bcast(b, row_ids, span):
    parts = [jnp.broadcast_to(b[r:r + 1, :], (span, b.shape[1])) for r in row_ids]
    return parts[0] if len(parts) == 1 else jnp.concatenate(parts, axis=0)


def _hgrn_kernel(lbl_ref, nw_ref, hg_ref, y_ref, st_ref):
    t_blk = HGRN_BLOCK
    dk = HGRN_DK
    sub = HGRN_SUB
    n_sub = t_blk // sub

    @pl.when(pl.program_id(1) == 0)
    def _():
        st_ref[...] = jnp.zeros_like(st_ref)

    lbl = lbl_ref[...]
    mx = jnp.max(lbl, axis=0, keepdims=True)
    e = jnp.exp(lbl - mx)
    lb_all = e[0:1, :] / jnp.sum(e, axis=0, keepdims=True)

    row = lax.broadcasted_iota(jnp.int32, (t_blk, t_blk), 0)
    col = lax.broadcasted_iota(jnp.int32, (t_blk, t_blk), 1)
    tri = (col <= row).astype(BF16)
    rid = lax.broadcasted_iota(jnp.int32, (t_blk, dk), 0)
    tile_sel = ((lax.broadcasted_iota(jnp.int32, (sub, t_blk), 1) & (sub - 1))
                == lax.broadcasted_iota(jnp.int32, (sub, t_blk), 0)).astype(BF16)
    same_sub = (row & -sub) == (col & -sub)
    t3 = lax.broadcasted_iota(jnp.int32, (n_sub, sub, sub, dk), 1)
    s3 = lax.broadcasted_iota(jnp.int32, (n_sub, sub, sub, dk), 2)
    causal3 = s3 <= t3

    for h in range(HGRN_HEADS):
        sl = slice(h * dk, (h + 1) * dk)
        w = HGRN_WIDTH
        qr = hg_ref[0, :, h * dk:(h + 1) * dk]
        fl = hg_ref[0, :, w + h * dk:w + (h + 1) * dk]
        vi = hg_ref[0, :, 2 * w + h * dk:2 * w + (h + 1) * dk]
        gr = hg_ref[0, :, 3 * w + h * dk:3 * w + (h + 1) * dk]
        lb = lb_all[:, sl]
        f = lb + (1.0 - lb) * _sigmoid(fl)
        lf = jnp.log(f)
        key = 1.0 - f
        qf = qr * _sigmoid(qr)
        v_bf = vi.astype(BF16)

        hi = lf.astype(BF16)
        r1 = lf - hi.astype(F32)
        mid = r1.astype(BF16)
        lo = (r1 - mid.astype(F32)).astype(BF16)
        b = _dot(tri, hi) + _dot(tri, mid) + _dot(tri, lo)

        b_last = b[t_blk - 1:t_blk, :]
        st = st_ref[h]
        q_inter = (qf * jnp.exp(b)).astype(BF16)
        o = _dot_nt(q_inter, st.astype(BF16))
        k_upd = (key * jnp.exp(b_last - b)).astype(BF16)
        st_ref[h] = st * jnp.exp(b_last) + _dot_tn(v_bf, k_upd)

        a_mat = jnp.zeros((t_blk, t_blk), F32)
        half = t_blk // 2
        while half >= sub:
            span = 2 * half
            ref_rows = [p * span + half - 1 for p in range(t_blk // span)]
            d = b - _rows_bcast(b, ref_rows, span)
            upper = (rid & half) != 0
            e_lvl = jnp.exp(jnp.where(upper, d, -d))
            q_hat = jnp.where(upper, qf * e_lvl, 0.0).astype(BF16)
            k_hat = jnp.where(upper, 0.0, key * e_lvl).astype(BF16)
            a_lvl = _dot_nt(q_hat, k_hat)
            a_mat = a_mat + jnp.where((row & -span) == (col & -span), a_lvl, 0.0)
            half //= 2

        b4 = b.reshape(n_sub, sub, dk)
        diff = b4[:, :, None, :] - b4[:, None, :, :]
        dec = jnp.exp(jnp.where(causal3, diff, -jnp.inf))
        prod = (qf.reshape(n_sub, sub, dk)[:, :, None, :] * dec) * key.reshape(n_sub, sub, dk)[:, None, :, :]
        a_diag = jnp.sum(prod, axis=-1).reshape(t_blk, sub)
        a_tiled = _dot(a_diag.astype(BF16), tile_sel)
        a_mat = a_mat + jnp.where(same_sub, a_tiled, 0.0)

        o = o + _dot(a_mat.astype(BF16), v_bf)
        o = _rms_scale(o) * nw_ref[:, sl]
        y_ref[0, :, sl] = (o * (gr * _sigmoid(gr))).astype(BF16)


def _hgrn2(hg, lb_logits, norm_w, batch, seq):
    hg3 = hg.reshape(batch, seq, 4 * HGRN_WIDTH)
    y = pl.pallas_call(
        _hgrn_kernel,
        grid=(batch, seq // HGRN_BLOCK),
        in_specs=[_resident(lb_logits.shape), _resident((1, HGRN_WIDTH)),
                  pl.BlockSpec((1, HGRN_BLOCK, 4 * HGRN_WIDTH), lambda b, t: (b, t, 0))],
        out_specs=pl.BlockSpec((1, HGRN_BLOCK, HGRN_WIDTH), lambda b, t: (b, t, 0)),
        out_shape=jax.ShapeDtypeStruct((batch, seq, HGRN_WIDTH), BF16),
        scratch_shapes=[pltpu.VMEM((HGRN_HEADS, HGRN_DK, HGRN_DK), F32)],
        compiler_params=pltpu.CompilerParams(
            dimension_semantics=("arbitrary", "arbitrary"), vmem_limit_bytes=VMEM_LIMIT),
        name="hgrn2",
    )(lb_logits, norm_w, hg3)
    return y.reshape(batch * seq, HGRN_WIDTH)


def _out_proj_kernel(o1, o2, o3, l1, l2, l3, yb_ref, x_ref, w_ref, h_ref):
    la, lb, lc = l1[...], l2[...], l3[...]
    m = jnp.maximum(jnp.maximum(la, lb), lc)
    ea, eb, ec = jnp.exp(la - m), jnp.exp(lb - m), jnp.exp(lc - m)
    ya = (ea * o1[...] + eb * o2[...] + ec * o3[...]) / (ea + eb + ec)
    a = ATTN_WIDTH
    h_ref[...] = (x_ref[...] + _dot(ya.astype(BF16), w_ref[0:a, :])
                  + _dot(yb_ref[...], w_ref[a:, :]))


def _out_proj(outs, lses, yb, x2, w_out_bf, tm=512):
    rows = x2.shape[0]
    row_spec = lambda width: pl.BlockSpec((tm, width), lambda i: (i, 0))
    return pl.pallas_call(
        _out_proj_kernel,
        grid=(rows // tm,),
        in_specs=[row_spec(ATTN_WIDTH)] * 7 + [row_spec(D_MODEL),
                                               _resident((ATTN_WIDTH + HGRN_WIDTH, D_MODEL))],
        out_specs=row_spec(D_MODEL),
        out_shape=jax.ShapeDtypeStruct((rows, D_MODEL), F32),
        compiler_params=pltpu.CompilerParams(
            dimension_semantics=("arbitrary",), vmem_limit_bytes=VMEM_LIMIT),
        name="merge_out_proj",
    )(*outs, *lses, yb, x2, w_out_bf)


def _ffn_kernel(h_ref, n2_ref, wgu_ref, wd_ref, nf_ref, out_ref, *, chunk):
    h = h_ref[...]
    u = (_rms_scale(h) * n2_ref[...]).astype(BF16)
    acc = h
    for c in range(FFN_HIDDEN // chunk):
        gate = _dot(u, wgu_ref[:, c * chunk:(c + 1) * chunk])
        up = _dot(u, wgu_ref[:, FFN_HIDDEN + c * chunk:FFN_HIDDEN + (c + 1) * chunk])
        act = (gate * _sigmoid(gate) * up).astype(BF16)
        acc = acc + _dot(act, wd_ref[c * chunk:(c + 1) * chunk, :])
    out_ref[...] = _rms_scale(acc) * nf_ref[...]


def _ffn(h, norm2_w, w_gu_bf, w_down_bf, final_w, tm=256, chunk=256):
    rows = h.shape[0]
    row_spec = pl.BlockSpec((tm, D_MODEL), lambda i: (i, 0))
    return pl.pallas_call(
        functools.partial(_ffn_kernel, chunk=chunk),
        grid=(rows // tm,),
        in_specs=[row_spec, _resident((1, D_MODEL)), _resident((D_MODEL, 2 * FFN_HIDDEN)),
                  _resident((FFN_HIDDEN, D_MODEL)), _resident((1, D_MODEL))],
        out_specs=row_spec,
        out_shape=jax.ShapeDtypeStruct((rows, D_MODEL), F32),
        compiler_params=pltpu.CompilerParams(
            dimension_semantics=("arbitrary",), vmem_limit_bytes=VMEM_LIMIT),
        name="ffn",
    )(h, norm2_w, w_gu_bf, w_down_bf, final_w)


def kernel(x, norm1_w, w_in, lb_logits, hgrn_norm_w, w_out, norm2_w, w_gate_up, w_down, final_norm_w):
    batch, seq, d_model = x.shape
    assert d_model == D_MODEL and norm1_w.shape[0] == 1
    assert all(w // d == ATTN_BLOCK for w, d in DILATED_PAIRS)
    x2 = x.reshape(batch * seq, d_model)
    cos_t, sin_t = _rope_tables(seq)
    q, k, v, hg = _in_proj(x2, norm1_w, w_in[0].astype(BF16), cos_t, sin_t, seq)
    outs, lses = [], []
    for _, dilation in DILATED_PAIRS:
        o, l = _dilated_attention(q, k, v, batch, seq, dilation)
        outs.append(o)
        lses.append(l)
    yb = _hgrn2(hg, lb_logits, hgrn_norm_w, batch, seq)
    h = _out_proj(outs, lses, yb, x2, w_out[0].astype(BF16))
    out = _ffn(h, norm2_w, w_gate_up[0].astype(BF16), w_down[0].astype(BF16),
               final_norm_w.reshape(1, d_model))
    return out.reshape(batch, seq, d_model)
```

```python
import functools

import numpy as np
import jax
import jax.numpy as jnp
from jax import lax
from jax.experimental import pallas as pl
from jax.experimental.pallas import tpu as pltpu

D_MODEL = 1024
ATTN_WIDTH = 512
ATTN_HEADS = 8
HEAD_DIM = 64
DILATED_PAIRS = ((128, 1), (512, 4), (2048, 16))
ATTN_BLOCK = 128
ROPE_THETA = 10000.0
HGRN_WIDTH = 512
HGRN_DK = 128
HGRN_HEADS = 4
FFN_HIDDEN = 2816
IN_PROJ_WIDTH = 3 * ATTN_WIDTH + 4 * HGRN_WIDTH
NORM_EPS = 1e-6

LANES = 128
PLANES = 16
HGRN_BLOCK = 128
HGRN_SUB = 16
VMEM_LIMIT = 56 * 1024 * 1024

F32 = jnp.float32
BF16 = jnp.bfloat16


def _dot(a, b):
    return jnp.dot(a, b, preferred_element_type=F32)


def _dot_nt(a, b):
    return lax.dot_general(a, b, (((1,), (1,)), ((), ())), preferred_element_type=F32)


def _dot_tn(a, b):
    return lax.dot_general(a, b, (((0,), (0,)), ((), ())), preferred_element_type=F32)


def _rms_scale(x):
    return x * lax.rsqrt(jnp.mean(x * x, axis=-1, keepdims=True) + NORM_EPS)


def _sigmoid(x):
    return 1.0 / (1.0 + jnp.exp(-x))


def _resident(shape):
    return pl.BlockSpec(shape, lambda *_: (0,) * len(shape), pipeline_mode=pl.Buffered(1))


def _rope_tables(seq):
    half = HEAD_DIM // 2
    inv_freq = ROPE_THETA ** (-np.arange(half, dtype=np.float64) / half)
    ang = np.arange(seq, dtype=np.float64)[:, None] * inv_freq[None, :]
    cos, sin = np.cos(ang), np.sin(ang)
    cos_t = np.concatenate([cos, cos, cos, cos], axis=1)
    sin_t = np.concatenate([-sin, sin, -sin, sin], axis=1)
    return jnp.asarray(cos_t, F32), jnp.asarray(sin_t, F32)


def _rotate_half_pairs(t, cos, sin):
    lane = lax.broadcasted_iota(jnp.int32, t.shape, 1)
    fwd = pltpu.roll(t, LANES - HEAD_DIM // 2, 1)
    bwd = pltpu.roll(t, HEAD_DIM // 2, 1)
    partner = jnp.where((lane & (HEAD_DIM // 2)) == 0, fwd, bwd)
    return t * cos + partner * sin


def _in_proj_kernel(x_ref, nw_ref, w_ref, cos_ref, sin_ref, q_ref, k_ref, v_ref, hg_ref, stage_ref):
    u = (_rms_scale(x_ref[0]) * nw_ref[...]).astype(BF16)
    cos = cos_ref[...]
    sin = sin_ref[...]
    a = ATTN_WIDTH
    n_pair = a // LANES
    pq = _dot(u, w_ref[:, 0:a])
    pk = _dot(u, w_ref[:, a:2 * a])
    pv = _dot(u, w_ref[:, 2 * a:3 * a])
    scale = HEAD_DIM ** -0.5
    for j in range(n_pair):
        sl = slice(j * LANES, (j + 1) * LANES)
        stage_ref[j] = _rotate_half_pairs(pq[:, sl], cos, sin) * scale
        stage_ref[n_pair + j] = _rotate_half_pairs(pk[:, sl], cos, sin)
        stage_ref[2 * n_pair + j] = pv[:, sl]
    rows_per_plane = stage_ref.shape[1] // PLANES
    for i, out_ref in enumerate((q_ref, k_ref, v_ref)):
        for j in range(n_pair):
            for r in range(PLANES):
                out_ref[0, j, r] = stage_ref[i * n_pair + j, pl.ds(r, rows_per_plane, stride=PLANES), :]
    hg_ref[0] = _dot(u, w_ref[:, 3 * a:])


def _in_proj(x, norm_w, w_in_bf, cos_t, sin_t, tm=512):
    batch, seq, _ = x.shape
    n_pair = ATTN_WIDTH // LANES
    row_spec = lambda width: pl.BlockSpec((1, tm, width), lambda b, i: (b, i, 0))
    tab_spec = pl.BlockSpec((tm, LANES), lambda b, i: (i, 0))
    plane_spec = pl.BlockSpec((1, n_pair, PLANES, tm // PLANES, LANES), lambda b, i: (b, 0, 0, i, 0))
    plane_shape = jax.ShapeDtypeStruct((batch, n_pair, PLANES, seq // PLANES, LANES), F32)
    return pl.pallas_call(
        _in_proj_kernel,
        grid=(batch, seq // tm),
        in_specs=[row_spec(D_MODEL), _resident((1, D_MODEL)),
                  _resident((D_MODEL, IN_PROJ_WIDTH)), tab_spec, tab_spec],
        out_specs=[plane_spec] * 3 + [row_spec(4 * HGRN_WIDTH)],
        out_shape=[plane_shape] * 3 + [jax.ShapeDtypeStruct((batch, seq, 4 * HGRN_WIDTH), F32)],
        scratch_shapes=[pltpu.VMEM((3 * n_pair, tm, LANES), F32)],
        compiler_params=pltpu.CompilerParams(
            dimension_semantics=("arbitrary", "arbitrary"), vmem_limit_bytes=VMEM_LIMIT),
        name="in_proj",
    )(x, norm_w, w_in_bf, cos_t, sin_t)


def _attend_block(q, k_prev, k_cur, v_prev, v_cur, mask, head_a):
    blk = ATTN_BLOCK
    qb = q.astype(BF16)
    zero = jnp.zeros_like(qb)
    lhs = jnp.concatenate([jnp.where(head_a, qb, zero), jnp.where(head_a, zero, qb)], axis=0)
    keys = jnp.concatenate([k_prev, k_cur], axis=0).astype(BF16)
    vals = jnp.concatenate([v_prev, v_cur], axis=0).astype(BF16)
    s = jnp.where(mask, _dot_nt(lhs, keys), -jnp.inf)
    m = jnp.max(s, axis=-1, keepdims=True)
    p = jnp.exp(s - m).astype(BF16)
    head_a2 = lax.broadcasted_iota(jnp.int32, vals.shape, 1) < HEAD_DIM
    one = jnp.ones_like(vals)
    acc_a = _dot(p[:blk], jnp.where(head_a2, vals, one))
    acc_b = _dot(p[blk:], jnp.where(head_a2, one, vals))
    out = jnp.where(head_a, acc_a, acc_b)
    den = pltpu.roll(jnp.where(head_a, acc_b, acc_a), HEAD_DIM, 1)
    mx = jnp.where(head_a, m[:blk], m[blk:])
    return out / den, mx + jnp.log(den)


def _attn_kernel(q_ref, kp_ref, kc_ref, vp_ref, vc_ref, y_ref, o_scr, l_scr):
    blk = ATTN_BLOCK
    has_prev = pl.program_id(2) > 0
    head_a = lax.broadcasted_iota(jnp.int32, (blk, LANES), 1) < HEAD_DIM
    row = lax.broadcasted_iota(jnp.int32, (2 * blk, 2 * blk), 0) & (blk - 1)
    col = lax.broadcasted_iota(jnp.int32, (2 * blk, 2 * blk), 1)
    in_prev = col < blk
    col = col & (blk - 1)

    def window_mask(index_of_row, prev_offset):
        delta = index_of_row(row) - index_of_row(col)
        delta = jnp.where(in_prev, delta + prev_offset, delta)
        return jnp.logical_and(delta >= 0, delta <= blk)

    def gather(ref, planes, lo, size):
        parts = [ref[0, 0, r, lo:lo + size, :] for r in planes]
        return parts[0] if len(parts) == 1 else jnp.concatenate(parts, axis=0)

    for pat, dilation in enumerate((16, 4, 1)):
        per_plane = blk * dilation // PLANES
        n_groups = dilation
        if dilation == 16:
            index_of_row = lambda i: i
        elif dilation == 4:
            index_of_row = lambda i: ((i & 31) << 2) + (i >> 5)
        else:
            index_of_row = lambda i: ((i & 7) << 4) + (i >> 3)
        mask_mid = window_mask(index_of_row, blk)
        mask_first = window_mask(index_of_row, jnp.where(has_prev, blk, 4 * blk))
        for res in range(n_groups):
            planes = list(range(res, PLANES, dilation))
            for n in range(blk // per_plane):
                lo = n * per_plane
                q = gather(q_ref, planes, lo, per_plane)
                k_cur = gather(kc_ref, planes, lo, per_plane)
                v_cur = gather(vc_ref, planes, lo, per_plane)
                if n == 0:
                    k_prev = gather(kp_ref, planes, blk - per_plane, per_plane)
                    v_prev = gather(vp_ref, planes, blk - per_plane, per_plane)
                    mask = mask_first
                else:
                    k_prev = gather(kc_ref, planes, lo - per_plane, per_plane)
                    v_prev = gather(vc_ref, planes, lo - per_plane, per_plane)
                    mask = mask_mid
                out, lse = _attend_block(q, k_prev, k_cur, v_prev, v_cur, mask, head_a)
                for j, r in enumerate(planes):
                    o_scr[pat, r, lo:lo + per_plane, :] = out[j * per_plane:(j + 1) * per_plane]
                    l_scr[pat, r, lo:lo + per_plane, :] = lse[j * per_plane:(j + 1) * per_plane]

    for r in range(PLANES):
        la, lb, lc = l_scr[0, r], l_scr[1, r], l_scr[2, r]
        m = jnp.maximum(jnp.maximum(la, lb), lc)
        ea, eb, ec = jnp.exp(la - m), jnp.exp(lb - m), jnp.exp(lc - m)
        y = (ea * o_scr[0, r] + eb * o_scr[1, r] + ec * o_scr[2, r]) / (ea + eb + ec)
        y_ref[0, 0, pl.ds(r, blk, stride=PLANES), :] = y


def _dilated_attention(q, k, v):
    batch, n_pair, planes, rows, lanes = q.shape
    blk = ATTN_BLOCK
    tile = planes * blk
    cur = pl.BlockSpec((1, 1, planes, blk, lanes), lambda b, h, t: (b, h, 0, t, 0))
    prev = pl.BlockSpec((1, 1, planes, blk, lanes), lambda b, h, t: (b, h, 0, jnp.maximum(t - 1, 0), 0))
    return pl.pallas_call(
        _attn_kernel,
        grid=(batch, n_pair, rows // blk),
        in_specs=[cur, prev, cur, prev, cur],
        out_specs=pl.BlockSpec((1, 1, tile, lanes), lambda b, h, t: (b, h, t, 0)),
        out_shape=jax.ShapeDtypeStruct((batch, n_pair, planes * rows, lanes), F32),
        scratch_shapes=[pltpu.VMEM((len(DILATED_PAIRS), planes, blk, lanes), F32)] * 2,
        compiler_params=pltpu.CompilerParams(
            dimension_semantics=("arbitrary",) * 3, vmem_limit_bytes=VMEM_LIMIT),
        name="dilated_attn",
    )(q, k, k, v, v)


def _rows_bcast(b, row_ids, span):
    parts = [jnp.broadcast_to(b[r:r + 1, :], (span, b.shape[1])) for r in row_ids]
    return parts[0] if len(parts) == 1 else jnp.concatenate(parts, axis=0)


def _hgrn_kernel(lbl_ref, nw_ref, hg_ref, y_ref, st_ref):
    t_blk = HGRN_BLOCK
    dk = HGRN_DK
    sub = HGRN_SUB
    n_sub = t_blk // sub

    @pl.when(pl.program_id(1) == 0)
    def _():
        st_ref[...] = jnp.zeros_like(st_ref)

    lbl = lbl_ref[...]
    mx = jnp.max(lbl, axis=0, keepdims=True)
    e = jnp.exp(lbl - mx)
    lb_all = e[0:1, :] / jnp.sum(e, axis=0, keepdims=True)

    row = lax.broadcasted_iota(jnp.int32, (t_blk, t_blk), 0)
    col = lax.broadcasted_iota(jnp.int32, (t_blk, t_blk), 1)
    tri = (col <= row).astype(BF16)
    rid = lax.broadcasted_iota(jnp.int32, (t_blk, dk), 0)
    tile_sel = ((lax.broadcasted_iota(jnp.int32, (sub, t_blk), 1) & (sub - 1))
                == lax.broadcasted_iota(jnp.int32, (sub, t_blk), 0)).astype(BF16)
    same_sub = (row & -sub) == (col & -sub)
    t3 = lax.broadcasted_iota(jnp.int32, (n_sub, sub, sub, dk), 1)
    s3 = lax.broadcasted_iota(jnp.int32, (n_sub, sub, sub, dk), 2)
    causal3 = s3 <= t3

    for h in range(HGRN_HEADS):
        sl = slice(h * dk, (h + 1) * dk)
        w = HGRN_WIDTH
        qr = hg_ref[0, :, h * dk:(h + 1) * dk]
        fl = hg_ref[0, :, w + h * dk:w + (h + 1) * dk]
        vi = hg_ref[0, :, 2 * w + h * dk:2 * w + (h + 1) * dk]
        gr = hg_ref[0, :, 3 * w + h * dk:3 * w + (h + 1) * dk]
        lb = lb_all[:, sl]
        f = lb + (1.0 - lb) * _sigmoid(fl)
        lf = jnp.log(f)
        key = 1.0 - f
        qf = qr * _sigmoid(qr)
        v_bf = vi.astype(BF16)

        hi = lf.astype(BF16)
        r1 = lf - hi.astype(F32)
        mid = r1.astype(BF16)
        lo = (r1 - mid.astype(F32)).astype(BF16)
        b = _dot(tri, hi) + _dot(tri, mid) + _dot(tri, lo)

        b_last = b[t_blk - 1:t_blk, :]
        st = st_ref[h]
        q_inter = (qf * jnp.exp(b)).astype(BF16)
        o = _dot_nt(q_inter, st.astype(BF16))
        k_upd = (key * jnp.exp(b_last - b)).astype(BF16)
        st_ref[h] = st * jnp.exp(b_last) + _dot_tn(v_bf, k_upd)

        a_mat = jnp.zeros((t_blk, t_blk), F32)
        half = t_blk // 2
        while half >= sub:
            span = 2 * half
            ref_rows = [p * span + half - 1 for p in range(t_blk // span)]
            d = b - _rows_bcast(b, ref_rows, span)
            upper = (rid & half) != 0
            e_lvl = jnp.exp(jnp.where(upper, d, -d))
            q_hat = jnp.where(upper, qf * e_lvl, 0.0).astype(BF16)
            k_hat = jnp.where(upper, 0.0, key * e_lvl).astype(BF16)
            a_lvl = _dot_nt(q_hat, k_hat)
            a_mat = a_mat + jnp.where((row & -span) == (col & -span), a_lvl, 0.0)
            half //= 2

        b4 = b.reshape(n_sub, sub, dk)
        diff = b4[:, :, None, :] - b4[:, None, :, :]
        dec = jnp.exp(jnp.where(causal3, diff, -jnp.inf))
        prod = (qf.reshape(n_sub, sub, dk)[:, :, None, :] * dec) * key.reshape(n_sub, sub, dk)[:, None, :, :]
        a_diag = jnp.sum(prod, axis=-1).reshape(t_blk, sub)
        a_tiled = _dot(a_diag.astype(BF16), tile_sel)
        a_mat = a_mat + jnp.where(same_sub, a_tiled, 0.0)

        o = o + _dot(a_mat.astype(BF16), v_bf)
        o = _rms_scale(o) * nw_ref[:, sl]
        y_ref[0, :, sl] = (o * (gr * _sigmoid(gr))).astype(BF16)


def _hgrn2(hg3, lb_logits, norm_w):
    batch, seq, _ = hg3.shape
    return pl.pallas_call(
        _hgrn_kernel,
        grid=(batch, seq // HGRN_BLOCK),
        in_specs=[_resident(lb_logits.shape), _resident((1, HGRN_WIDTH)),
                  pl.BlockSpec((1, HGRN_BLOCK, 4 * HGRN_WIDTH), lambda b, t: (b, t, 0))],
        out_specs=pl.BlockSpec((1, HGRN_BLOCK, HGRN_WIDTH), lambda b, t: (b, t, 0)),
        out_shape=jax.ShapeDtypeStruct((batch, seq, HGRN_WIDTH), BF16),
        scratch_shapes=[pltpu.VMEM((HGRN_HEADS, HGRN_DK, HGRN_DK), F32)],
        compiler_params=pltpu.CompilerParams(
            dimension_semantics=("arbitrary", "arbitrary"), vmem_limit_bytes=VMEM_LIMIT),
        name="hgrn2",
    )(lb_logits, norm_w, hg3)


def _out_proj_kernel(ya_ref, yb_ref, x_ref, w_ref, h_ref):
    acc = x_ref[0] + _dot(yb_ref[0], w_ref[ATTN_WIDTH:, :])
    for j in range(ATTN_WIDTH // LANES):
        acc = acc + _dot(ya_ref[0, j].astype(BF16), w_ref[j * LANES:(j + 1) * LANES, :])
    h_ref[0] = acc


def _out_proj(ya, yb, x, w_out_bf, tm=512):
    batch, seq, _ = x.shape
    n_pair = ATTN_WIDTH // LANES
    row_spec = lambda width: pl.BlockSpec((1, tm, width), lambda b, i: (b, i, 0))
    return pl.pallas_call(
        _out_proj_kernel,
        grid=(batch, seq // tm),
        in_specs=[pl.BlockSpec((1, n_pair, tm, LANES), lambda b, i: (b, 0, i, 0)),
                  row_spec(HGRN_WIDTH), row_spec(D_MODEL),
                  _resident((ATTN_WIDTH + HGRN_WIDTH, D_MODEL))],
        out_specs=row_spec(D_MODEL),
        out_shape=jax.ShapeDtypeStruct((batch, seq, D_MODEL), F32),
        compiler_params=pltpu.CompilerParams(
            dimension_semantics=("arbitrary", "arbitrary"), vmem_limit_bytes=VMEM_LIMIT),
        name="out_proj",
    )(ya, yb, x, w_out_bf)


def _ffn_kernel(h_ref, n2_ref, wgu_ref, wd_ref, nf_ref, out_ref, *, chunk):
    h = h_ref[...]
    u = (_rms_scale(h) * n2_ref[...]).astype(BF16)
    acc = h
    for c in range(FFN_HIDDEN // chunk):
        gate = _dot(u, wgu_ref[:, c * chunk:(c + 1) * chunk])
        up = _dot(u, wgu_ref[:, FFN_HIDDEN + c * chunk:FFN_HIDDEN + (c + 1) * chunk])
        act = (gate * _sigmoid(gate) * up).astype(BF16)
        acc = acc + _dot(act, wd_ref[c * chunk:(c + 1) * chunk, :])
    out_ref[...] = _rms_scale(acc) * nf_ref[...]


def _ffn(h, norm2_w, w_gu_bf, w_down_bf, final_w, tm=256, chunk=256):
    rows = h.shape[0]
    row_spec = pl.BlockSpec((tm, D_MODEL), lambda i: (i, 0))
    return pl.pallas_call(
        functools.partial(_ffn_kernel, chunk=chunk),
        grid=(rows // tm,),
        in_specs=[row_spec, _resident((1, D_MODEL)), _resident((D_MODEL, 2 * FFN_HIDDEN)),
                  _resident((FFN_HIDDEN, D_MODEL)), _resident((1, D_MODEL))],
        out_specs=row_spec,
        out_shape=jax.ShapeDtypeStruct((rows, D_MODEL), F32),
        compiler_params=pltpu.CompilerParams(
            dimension_semantics=("arbitrary",), vmem_limit_bytes=VMEM_LIMIT),
        name="ffn",
    )(h, norm2_w, w_gu_bf, w_down_bf, final_w)


def kernel(x, norm1_w, w_in, lb_logits, hgrn_norm_w, w_out, norm2_w, w_gate_up, w_down, final_norm_w):
    batch, seq, d_model = x.shape
    assert d_model == D_MODEL and norm1_w.shape[0] == 1
    assert all(w // d == ATTN_BLOCK for w, d in DILATED_PAIRS)
    assert tuple(d for _, d in DILATED_PAIRS) == (1, 4, 16) and seq % (PLANES * ATTN_BLOCK) == 0
    cos_t, sin_t = _rope_tables(seq)
    q, k, v, hg = _in_proj(x, norm1_w, w_in[0].astype(BF16), cos_t, sin_t)
    ya = _dilated_attention(q, k, v)
    yb = _hgrn2(hg, lb_logits, hgrn_norm_w)
    h = _out_proj(ya, yb, x, w_out[0].astype(BF16))
    out = _ffn(h.reshape(batch * seq, d_model), norm2_w, w_gate_up[0].astype(BF16),
               w_down[0].astype(BF16), final_norm_w.reshape(1, d_model))
    return out.reshape(batch, seq, d_model)
```

```python
import functools

import numpy as np
import jax
import jax.numpy as jnp
from jax import lax
from jax.experimental import pallas as pl
from jax.experimental.pallas import tpu as pltpu

D_MODEL = 1024
ATTN_WIDTH = 512
ATTN_HEADS = 8
HEAD_DIM = 64
DILATED_PAIRS = ((128, 1), (512, 4), (2048, 16))
ATTN_BLOCK = 128
ROPE_THETA = 10000.0
HGRN_WIDTH = 512
HGRN_DK = 128
HGRN_HEADS = 4
FFN_HIDDEN = 2816
IN_PROJ_WIDTH = 3 * ATTN_WIDTH + 4 * HGRN_WIDTH
NORM_EPS = 1e-6

LANES = 128
PLANES = 16
HGRN_BLOCK = 128
HGRN_SUB = 8
VMEM_LIMIT = 56 * 1024 * 1024

F32 = jnp.float32
BF16 = jnp.bfloat16


def _dot(a, b):
    return jnp.dot(a, b, preferred_element_type=F32)


def _dot_nt(a, b):
    return lax.dot_general(a, b, (((1,), (1,)), ((), ())), preferred_element_type=F32)


def _dot_tn(a, b):
    return lax.dot_general(a, b, (((0,), (0,)), ((), ())), preferred_element_type=F32)


def _rms_scale(x):
    return x * lax.rsqrt(jnp.mean(x * x, axis=-1, keepdims=True) + NORM_EPS)


def _sigmoid(x):
    return 1.0 / (1.0 + jnp.exp(-x))


def _resident(shape):
    return pl.BlockSpec(shape, lambda *_: (0,) * len(shape), pipeline_mode=pl.Buffered(1))


def _rope_tables(seq):
    half = HEAD_DIM // 2
    inv_freq = ROPE_THETA ** (-np.arange(half, dtype=np.float64) / half)
    ang = np.arange(seq, dtype=np.float64)[:, None] * inv_freq[None, :]
    cos, sin = np.cos(ang), np.sin(ang)
    cos_t = np.concatenate([cos, cos, cos, cos], axis=1)
    sin_t = np.concatenate([-sin, sin, -sin, sin], axis=1)
    return jnp.asarray(cos_t, F32), jnp.asarray(sin_t, F32)


def _rotate_half_pairs(t, cos, sin):
    lane = lax.broadcasted_iota(jnp.int32, t.shape, 1)
    fwd = pltpu.roll(t, LANES - HEAD_DIM // 2, 1)
    bwd = pltpu.roll(t, HEAD_DIM // 2, 1)
    partner = jnp.where((lane & (HEAD_DIM // 2)) == 0, fwd, bwd)
    return t * cos + partner * sin


def _in_proj_kernel(x_ref, nw_ref, w_ref, cos_ref, sin_ref, q_ref, k_ref, v_ref, hg_ref, stage_ref):
    u = (_rms_scale(x_ref[0]) * nw_ref[...]).astype(BF16)
    cos = cos_ref[...]
    sin = sin_ref[...]
    a = ATTN_WIDTH
    n_pair = a // LANES
    pq = _dot(u, w_ref[:, 0:a])
    pk = _dot(u, w_ref[:, a:2 * a])
    pv = _dot(u, w_ref[:, 2 * a:3 * a])
    scale = HEAD_DIM ** -0.5
    for j in range(n_pair):
        sl = slice(j * LANES, (j + 1) * LANES)
        stage_ref[j] = _rotate_half_pairs(pq[:, sl], cos, sin) * scale
        stage_ref[n_pair + j] = _rotate_half_pairs(pk[:, sl], cos, sin)
        stage_ref[2 * n_pair + j] = pv[:, sl]
    rows_per_plane = stage_ref.shape[1] // PLANES
    for i, out_ref in enumerate((q_ref, k_ref, v_ref)):
        for j in range(n_pair):
            for r in range(PLANES):
                out_ref[0, j, r] = stage_ref[i * n_pair + j, pl.ds(r, rows_per_plane, stride=PLANES), :]
    hg_ref[0] = _dot(u, w_ref[:, 3 * a:])


def _in_proj(x, norm_w, w_in_bf, cos_t, sin_t, tm=512):
    batch, seq, _ = x.shape
    n_pair = ATTN_WIDTH // LANES
    row_spec = lambda width: pl.BlockSpec((1, tm, width), lambda b, i: (b, i, 0))
    tab_spec = pl.BlockSpec((tm, LANES), lambda b, i: (i, 0))
    plane_spec = pl.BlockSpec((1, n_pair, PLANES, tm // PLANES, LANES), lambda b, i: (b, 0, 0, i, 0))
    plane_shape = jax.ShapeDtypeStruct((batch, n_pair, PLANES, seq // PLANES, LANES), F32)
    return pl.pallas_call(
        _in_proj_kernel,
        grid=(batch, seq // tm),
        in_specs=[row_spec(D_MODEL), _resident((1, D_MODEL)),
                  _resident((D_MODEL, IN_PROJ_WIDTH)), tab_spec, tab_spec],
        out_specs=[plane_spec] * 3 + [row_spec(4 * HGRN_WIDTH)],
        out_shape=[plane_shape] * 3 + [jax.ShapeDtypeStruct((batch, seq, 4 * HGRN_WIDTH), F32)],
        scratch_shapes=[pltpu.VMEM((3 * n_pair, tm, LANES), F32)],
        compiler_params=pltpu.CompilerParams(
            dimension_semantics=("arbitrary", "arbitrary"), vmem_limit_bytes=VMEM_LIMIT),
        name="in_proj",
    )(x, norm_w, w_in_bf, cos_t, sin_t)


def _attend_block(q, k_prev, k_cur, v_prev, v_cur, mask, head_a):
    blk = ATTN_BLOCK
    qb = q.astype(BF16)
    zero = jnp.zeros_like(qb)
    lhs = jnp.concatenate([jnp.where(head_a, qb, zero), jnp.where(head_a, zero, qb)], axis=0)
    keys = jnp.concatenate([k_prev, k_cur], axis=0).astype(BF16)
    vals = jnp.concatenate([v_prev, v_cur], axis=0).astype(BF16)
    s = jnp.where(mask, _dot_nt(lhs, keys), -jnp.inf)
    m = jnp.max(s, axis=-1, keepdims=True)
    p = jnp.exp(s - m).astype(BF16)
    head_a2 = lax.broadcasted_iota(jnp.int32, vals.shape, 1) < HEAD_DIM
    one = jnp.ones_like(vals)
    acc_a = _dot(p[:blk], jnp.where(head_a2, vals, one))
    acc_b = _dot(p[blk:], jnp.where(head_a2, one, vals))
    out = jnp.where(head_a, acc_a, acc_b)
    den = pltpu.roll(jnp.where(head_a, acc_b, acc_a), HEAD_DIM, 1)
    mx = jnp.where(head_a, m[:blk], m[blk:])
    return out / den, mx + jnp.log(den)


def _attn_kernel(q_ref, kp_ref, kc_ref, vp_ref, vc_ref, y_ref, o_scr, l_scr):
    blk = ATTN_BLOCK
    has_prev = pl.program_id(2) > 0
    head_a = lax.broadcasted_iota(jnp.int32, (blk, LANES), 1) < HEAD_DIM
    row = lax.broadcasted_iota(jnp.int32, (2 * blk, 2 * blk), 0) & (blk - 1)
    col = lax.broadcasted_iota(jnp.int32, (2 * blk, 2 * blk), 1)
    in_prev = col < blk
    col = col & (blk - 1)

    def window_mask(index_of_row, prev_offset):
        delta = index_of_row(row) - index_of_row(col)
        delta = jnp.where(in_prev, delta + prev_offset, delta)
        return jnp.logical_and(delta >= 0, delta <= blk)

    def gather(ref, planes, lo, size):
        parts = [ref[0, 0, r, lo:lo + size, :] for r in planes]
        return parts[0] if len(parts) == 1 else jnp.concatenate(parts, axis=0)

    for pat, dilation in enumerate((16, 4, 1)):
        per_plane = blk * dilation // PLANES
        n_groups = dilation
        if dilation == 16:
            index_of_row = lambda i: i
        elif dilation == 4:
            index_of_row = lambda i: ((i & 31) << 2) + (i >> 5)
        else:
            index_of_row = lambda i: ((i & 7) << 4) + (i >> 3)
        mask_mid = window_mask(index_of_row, blk)
        mask_first = window_mask(index_of_row, jnp.where(has_prev, blk, 4 * blk))
        for res in range(n_groups):
            planes = list(range(res, PLANES, dilation))
            for n in range(blk // per_plane):
                lo = n * per_plane
                q = gather(q_ref, planes, lo, per_plane)
                k_cur = gather(kc_ref, planes, lo, per_plane)
                v_cur = gather(vc_ref, planes, lo, per_plane)
                if n == 0:
                    k_prev = gather(kp_ref, planes, blk - per_plane, per_plane)
                    v_prev = gather(vp_ref, planes, blk - per_plane, per_plane)
                    mask = mask_first
                else:
                    k_prev = gather(kc_ref, planes, lo - per_plane, per_plane)
                    v_prev = gather(vc_ref, planes, lo - per_plane, per_plane)
                    mask = mask_mid
                out, lse = _attend_block(q, k_prev, k_cur, v_prev, v_cur, mask, head_a)
                for j, r in enumerate(planes):
                    o_scr[pat, r, lo:lo + per_plane, :] = out[j * per_plane:(j + 1) * per_plane]
                    l_scr[pat, r, lo:lo + per_plane, :] = lse[j * per_plane:(j + 1) * per_plane]

    for r in range(PLANES):
        la, lb, lc = l_scr[0, r], l_scr[1, r], l_scr[2, r]
        m = jnp.maximum(jnp.maximum(la, lb), lc)
        ea, eb, ec = jnp.exp(la - m), jnp.exp(lb - m), jnp.exp(lc - m)
        y = (ea * o_scr[0, r] + eb * o_scr[1, r] + ec * o_scr[2, r]) / (ea + eb + ec)
        y_ref[0, 0, pl.ds(r, blk, stride=PLANES), :] = y


def _dilated_attention(q, k, v):
    batch, n_pair, planes, rows, lanes = q.shape
    blk = ATTN_BLOCK
    tile = planes * blk
    cur = pl.BlockSpec((1, 1, planes, blk, lanes), lambda b, h, t: (b, h, 0, t, 0))
    prev = pl.BlockSpec((1, 1, planes, blk, lanes), lambda b, h, t: (b, h, 0, jnp.maximum(t - 1, 0), 0))
    return pl.pallas_call(
        _attn_kernel,
        grid=(batch, n_pair, rows // blk),
        in_specs=[cur, prev, cur, prev, cur],
        out_specs=pl.BlockSpec((1, 1, tile, lanes), lambda b, h, t: (b, h, t, 0)),
        out_shape=jax.ShapeDtypeStruct((batch, n_pair, planes * rows, lanes), F32),
        scratch_shapes=[pltpu.VMEM((len(DILATED_PAIRS), planes, blk, lanes), F32)] * 2,
        compiler_params=pltpu.CompilerParams(
            dimension_semantics=("arbitrary",) * 3, vmem_limit_bytes=VMEM_LIMIT),
        name="dilated_attn",
    )(q, k, k, v, v)


def _hgrn_block(h, base, lb, nw_ref, hg_ref, y_ref, st_ref, b_scr, q_scr, k_scr, a_scr, tri, pair_key):
    t_blk = HGRN_BLOCK
    dk = HGRN_DK
    sub = HGRN_SUB
    w = HGRN_WIDTH
    rows = pl.ds(base, t_blk)
    sl = slice(h * dk, (h + 1) * dk)
    qr = hg_ref[0, rows, h * dk:(h + 1) * dk]
    fl = hg_ref[0, rows, w + h * dk:w + (h + 1) * dk]
    vi = hg_ref[0, rows, 2 * w + h * dk:2 * w + (h + 1) * dk]
    gr = hg_ref[0, rows, 3 * w + h * dk:3 * w + (h + 1) * dk]
    f = lb + (1.0 - lb) * _sigmoid(fl)
    lf = jnp.log2(f)
    key = 1.0 - f
    qf = qr * _sigmoid(qr)
    v_bf = vi.astype(BF16)

    hi = lf.astype(BF16)
    r1 = lf - hi.astype(F32)
    mid = r1.astype(BF16)
    lo = (r1 - mid.astype(F32)).astype(BF16)
    b = _dot(tri, hi) + _dot(tri, mid) + _dot(tri, lo)
    b_scr[...] = b
    q_scr[...] = qf
    k_scr[...] = jnp.log2(key) - b

    b_last = b_scr[t_blk - 1:t_blk, :]
    st = st_ref[h]
    q_inter = (qf * jnp.exp2(b)).astype(BF16)
    o = _dot_nt(q_inter, st.astype(BF16))
    k_upd = (key * jnp.exp2(b_last - b)).astype(BF16)
    st_ref[h] = st * jnp.exp2(b_last) + _dot_tn(v_bf, k_upd)

    sub_id = lax.broadcasted_iota(jnp.int32, (sub, dk), 0)
    lane_id = lax.broadcasted_iota(jnp.int32, (sub, t_blk), 1)
    for c in range(t_blk // sub):
        lo_row = c * sub
        b_t = b_scr[lo_row:lo_row + sub, :]
        q_t = q_scr[lo_row:lo_row + sub, :]
        blk_rows = jnp.zeros((sub, t_blk), F32)
        for s in range(sub):
            c_s = k_scr[lo_row + s:lo_row + s + 1, :]
            dec_k = jnp.exp2(jnp.where(sub_id >= s, b_t + c_s, -jnp.inf))
            col = jnp.sum(q_t * dec_k, axis=-1, keepdims=True)
            blk_rows = jnp.where(lane_id == lo_row + s, col, blk_rows)
        a_scr[lo_row:lo_row + sub, :] = blk_rows

    a_mat = a_scr[...]
    rid = lax.broadcasted_iota(jnp.int32, (t_blk, dk), 0)
    half = sub
    while half < t_blk:
        span = 2 * half
        refs = [jnp.broadcast_to(b_scr[p * span + half - 1:p * span + half, :], (span, dk))
                for p in range(t_blk // span)]
        d = b - (refs[0] if len(refs) == 1 else jnp.concatenate(refs, axis=0))
        upper = (rid & half) != 0
        x = (jnp.where(upper, qf, key) * jnp.exp2(jnp.where(upper, d, -d))).astype(BF16)
        a_mat = jnp.where(pair_key == half, _dot_nt(x, x), a_mat)
        half = span

    o = o + _dot(a_mat.astype(BF16), v_bf)
    o = _rms_scale(o) * nw_ref[:, sl]
    y_ref[0, rows, sl] = (o * (gr * _sigmoid(gr))).astype(BF16)


def _hgrn_kernel(lbl_ref, nw_ref, hg_ref, y_ref, st_ref, b_scr, q_scr, k_scr, a_scr):
    t_blk = HGRN_BLOCK

    @pl.when(pl.program_id(1) == 0)
    def _():
        st_ref[...] = jnp.zeros_like(st_ref)

    lbl = lbl_ref[...]
    mx = jnp.max(lbl, axis=0, keepdims=True)
    e = jnp.exp(lbl - mx)
    lb_all = e[0:1, :] / jnp.sum(e, axis=0, keepdims=True)

    row = lax.broadcasted_iota(jnp.int32, (t_blk, t_blk), 0)
    col = lax.broadcasted_iota(jnp.int32, (t_blk, t_blk), 1)
    tri = (col <= row).astype(BF16)
    pair_key = jnp.zeros((t_blk, t_blk), jnp.int32)
    half = HGRN_SUB
    while half < t_blk:
        pair_key = jnp.where(jnp.logical_and((row ^ col) >= half, col < row), half, pair_key)
        half *= 2

    def body(i, carry):
        base = pl.multiple_of(i * t_blk, t_blk)
        for h in range(HGRN_HEADS):
            lb = lb_all[:, h * HGRN_DK:(h + 1) * HGRN_DK]
            _hgrn_block(h, base, lb, nw_ref, hg_ref, y_ref, st_ref, b_scr, q_scr, k_scr, a_scr,
                        tri, pair_key)
        return carry

    lax.fori_loop(0, hg_ref.shape[1] // t_blk, body, 0)


def _hgrn2(hg3, lb_logits, norm_w, rows_per_step=512):
    batch, seq, _ = hg3.shape
    blk_scratch = pltpu.VMEM((HGRN_BLOCK, HGRN_DK), F32)
    return pl.pallas_call(
        _hgrn_kernel,
        grid=(batch, seq // rows_per_step),
        in_specs=[_resident(lb_logits.shape), _resident((1, HGRN_WIDTH)),
                  pl.BlockSpec((1, rows_per_step, 4 * HGRN_WIDTH), lambda b, t: (b, t, 0))],
        out_specs=pl.BlockSpec((1, rows_per_step, HGRN_WIDTH), lambda b, t: (b, t, 0)),
        out_shape=jax.ShapeDtypeStruct((batch, seq, HGRN_WIDTH), BF16),
        scratch_shapes=[pltpu.VMEM((HGRN_HEADS, HGRN_DK, HGRN_DK), F32),
                        blk_scratch, blk_scratch, blk_scratch,
                        pltpu.VMEM((HGRN_BLOCK, HGRN_BLOCK), F32)],
        compiler_params=pltpu.CompilerParams(
            dimension_semantics=("arbitrary", "arbitrary"), vmem_limit_bytes=VMEM_LIMIT),
        name="hgrn2",
    )(lb_logits, norm_w, hg3)


def _out_proj_kernel(ya_ref, yb_ref, x_ref, w_ref, h_ref):
    acc = x_ref[0] + _dot(yb_ref[0], w_ref[ATTN_WIDTH:, :])
    for j in range(ATTN_WIDTH // LANES):
        acc = acc + _dot(ya_ref[0, j].astype(BF16), w_ref[j * LANES:(j + 1) * LANES, :])
    h_ref[0] = acc


def _out_proj(ya, yb, x, w_out_bf, tm=512):
    batch, seq, _ = x.shape
    n_pair = ATTN_WIDTH // LANES
    row_spec = lambda width: pl.BlockSpec((1, tm, width), lambda b, i: (b, i, 0))
    return pl.pallas_call(
        _out_proj_kernel,
        grid=(batch, seq // tm),
        in_specs=[pl.BlockSpec((1, n_pair, tm, LANES), lambda b, i: (b, 0, i, 0)),
                  row_spec(HGRN_WIDTH), row_spec(D_MODEL),
                  _resident((ATTN_WIDTH + HGRN_WIDTH, D_MODEL))],
        out_specs=row_spec(D_MODEL),
        out_shape=jax.ShapeDtypeStruct((batch, seq, D_MODEL), F32),
        compiler_params=pltpu.CompilerParams(
            dimension_semantics=("arbitrary", "arbitrary"), vmem_limit_bytes=VMEM_LIMIT),
        name="out_proj",
    )(ya, yb, x, w_out_bf)


def _ffn_kernel(h_ref, n2_ref, wgu_ref, wd_ref, nf_ref, out_ref, *, chunk):
    h = h_ref[...]
    u = (_rms_scale(h) * n2_ref[...]).astype(BF16)
    acc = h
    for c in range(FFN_HIDDEN // chunk):
        gate = _dot(u, wgu_ref[:, c * chunk:(c + 1) * chunk])
        up = _dot(u, wgu_ref[:, FFN_HIDDEN + c * chunk:FFN_HIDDEN + (c + 1) * chunk])
        act = (gate * _sigmoid(gate) * up).astype(BF16)
        acc = acc + _dot(act, wd_ref[c * chunk:(c + 1) * chunk, :])
    out_ref[...] = _rms_scale(acc) * nf_ref[...]


def _ffn(h, norm2_w, w_gu_bf, w_down_bf, final_w, tm=512, chunk=256):
    rows = h.shape[0]
    row_spec = pl.BlockSpec((tm, D_MODEL), lambda i: (i, 0))
    return pl.pallas_call(
        functools.partial(_ffn_kernel, chunk=chunk),
        grid=(rows // tm,),
        in_specs=[row_spec, _resident((1, D_MODEL)), _resident((D_MODEL, 2 * FFN_HIDDEN)),
                  _resident((FFN_HIDDEN, D_MODEL)), _resident((1, D_MODEL))],
        out_specs=row_spec,
        out_shape=jax.ShapeDtypeStruct((rows, D_MODEL), F32),
        compiler_params=pltpu.CompilerParams(
            dimension_semantics=("arbitrary",), vmem_limit_bytes=VMEM_LIMIT),
        name="ffn",
    )(h, norm2_w, w_gu_bf, w_down_bf, final_w)


def kernel(x, norm1_w, w_in, lb_logits, hgrn_norm_w, w_out, norm2_w, w_gate_up, w_down, final_norm_w):
    batch, seq, d_model = x.shape
    assert d_model == D_MODEL and norm1_w.shape[0] == 1
    assert all(w // d == ATTN_BLOCK for w, d in DILATED_PAIRS)
    assert tuple(d for _, d in DILATED_PAIRS) == (1, 4, 16) and seq % (PLANES * ATTN_BLOCK) == 0
    cos_t, sin_t = _rope_tables(seq)
    q, k, v, hg = _in_proj(x, norm1_w, w_in[0].astype(BF16), cos_t, sin_t)
    ya = _dilated_attention(q, k, v)
    yb = _hgrn2(hg, lb_logits, hgrn_norm_w)
    h = _out_proj(ya, yb, x, w_out[0].astype(BF16))
    out = _ffn(h.reshape(batch * seq, d_model), norm2_w, w_gate_up[0].astype(BF16),
               w_down[0].astype(BF16), final_norm_w.reshape(1, d_model))
    return out.reshape(batch, seq, d_model)
```

```python
import functools

import numpy as np
import jax
import jax.numpy as jnp
from jax import lax
from jax.experimental import pallas as pl
from jax.experimental.pallas import tpu as pltpu

D_MODEL = 1024
ATTN_WIDTH = 512
ATTN_HEADS = 8
HEAD_DIM = 64
DILATED_PAIRS = ((128, 1), (512, 4), (2048, 16))
ATTN_BLOCK = 128
ROPE_THETA = 10000.0
HGRN_WIDTH = 512
HGRN_DK = 128
HGRN_HEADS = 4
FFN_HIDDEN = 2816
IN_PROJ_WIDTH = 3 * ATTN_WIDTH + 4 * HGRN_WIDTH
NORM_EPS = 1e-6

LANES = 128
PLANES = 16
HGRN_BLOCK = 128
HGRN_SUB = 8
VMEM_LIMIT = 56 * 1024 * 1024

F32 = jnp.float32
BF16 = jnp.bfloat16


def _dot(a, b):
    return jnp.dot(a, b, preferred_element_type=F32)


def _dot_nt(a, b):
    return lax.dot_general(a, b, (((1,), (1,)), ((), ())), preferred_element_type=F32)


def _dot_tn(a, b):
    return lax.dot_general(a, b, (((0,), (0,)), ((), ())), preferred_element_type=F32)


def _rms_scale(x):
    return x * lax.rsqrt(jnp.mean(x * x, axis=-1, keepdims=True) + NORM_EPS)


def _sigmoid(x):
    return 1.0 / (1.0 + jnp.exp(-x))


def _resident(shape):
    return pl.BlockSpec(shape, lambda *_: (0,) * len(shape), pipeline_mode=pl.Buffered(1))


def _rope_tables(seq):
    half = HEAD_DIM // 2
    inv_freq = ROPE_THETA ** (-np.arange(half, dtype=np.float64) / half)
    ang = np.arange(seq, dtype=np.float64)[:, None] * inv_freq[None, :]
    cos, sin = np.cos(ang), np.sin(ang)
    cos_t = np.concatenate([cos, cos, cos, cos], axis=1)
    sin_t = np.concatenate([-sin, sin, -sin, sin], axis=1)
    return jnp.asarray(cos_t, F32), jnp.asarray(sin_t, F32)


def _rotate_half_pairs(t, cos, sin):
    lane = lax.broadcasted_iota(jnp.int32, t.shape, 1)
    fwd = pltpu.roll(t, LANES - HEAD_DIM // 2, 1)
    bwd = pltpu.roll(t, HEAD_DIM // 2, 1)
    partner = jnp.where((lane & (HEAD_DIM // 2)) == 0, fwd, bwd)
    return t * cos + partner * sin


def _in_proj_kernel(x_ref, nw_ref, w_ref, cos_ref, sin_ref, q_ref, k_ref, v_ref, hg_ref, stage_ref):
    u = (_rms_scale(x_ref[0]) * nw_ref[...]).astype(BF16)
    cos = cos_ref[...]
    sin = sin_ref[...]
    a = ATTN_WIDTH
    n_pair = a // LANES
    pq = _dot(u, w_ref[0, :, 0:a].astype(BF16))
    pk = _dot(u, w_ref[0, :, a:2 * a].astype(BF16))
    pv = _dot(u, w_ref[0, :, 2 * a:3 * a].astype(BF16))
    scale = HEAD_DIM ** -0.5
    for j in range(n_pair):
        sl = slice(j * LANES, (j + 1) * LANES)
        stage_ref[j] = _rotate_half_pairs(pq[:, sl], cos, sin) * scale
        stage_ref[n_pair + j] = _rotate_half_pairs(pk[:, sl], cos, sin)
        stage_ref[2 * n_pair + j] = pv[:, sl]
    rows_per_plane = stage_ref.shape[1] // PLANES
    for i, out_ref in enumerate((q_ref, k_ref, v_ref)):
        for j in range(n_pair):
            for r in range(PLANES):
                out_ref[0, j, r] = stage_ref[i * n_pair + j, pl.ds(r, rows_per_plane, stride=PLANES), :]
    hg_ref[0] = _dot(u, w_ref[0, :, 3 * a:].astype(BF16))


def _in_proj(x, norm_w, w_in, cos_t, sin_t, tm=512):
    batch, seq, _ = x.shape
    n_pair = ATTN_WIDTH // LANES
    row_spec = lambda width: pl.BlockSpec((1, tm, width), lambda b, i: (b, i, 0))
    tab_spec = pl.BlockSpec((tm, LANES), lambda b, i: (i, 0))
    plane_spec = pl.BlockSpec((1, n_pair, PLANES, tm // PLANES, LANES), lambda b, i: (b, 0, 0, i, 0))
    plane_shape = jax.ShapeDtypeStruct((batch, n_pair, PLANES, seq // PLANES, LANES), F32)
    return pl.pallas_call(
        _in_proj_kernel,
        grid=(batch, seq // tm),
        in_specs=[row_spec(D_MODEL), _resident((1, D_MODEL)),
                  _resident(w_in.shape), tab_spec, tab_spec],
        out_specs=[plane_spec] * 3 + [row_spec(4 * HGRN_WIDTH)],
        out_shape=[plane_shape] * 3 + [jax.ShapeDtypeStruct((batch, seq, 4 * HGRN_WIDTH), F32)],
        scratch_shapes=[pltpu.VMEM((3 * n_pair, tm, LANES), F32)],
        compiler_params=pltpu.CompilerParams(
            dimension_semantics=("arbitrary", "arbitrary"), vmem_limit_bytes=VMEM_LIMIT),
        name="in_proj",
    )(x, norm_w, w_in, cos_t, sin_t)


def _attend_block(q, k_prev, k_cur, v_prev, v_cur, mask, head_a):
    blk = ATTN_BLOCK
    qb = q.astype(BF16)
    zero = jnp.zeros_like(qb)
    lhs = jnp.concatenate([jnp.where(head_a, qb, zero), jnp.where(head_a, zero, qb)], axis=0)
    keys = jnp.concatenate([k_prev, k_cur], axis=0).astype(BF16)
    vals = jnp.concatenate([v_prev, v_cur], axis=0).astype(BF16)
    s = jnp.where(mask, _dot_nt(lhs, keys), -jnp.inf)
    m = jnp.max(s, axis=-1, keepdims=True)
    p = jnp.exp(s - m).astype(BF16)
    head_a2 = lax.broadcasted_iota(jnp.int32, vals.shape, 1) < HEAD_DIM
    one = jnp.ones_like(vals)
    acc_a = _dot(p[:blk], jnp.where(head_a2, vals, one))
    acc_b = _dot(p[blk:], jnp.where(head_a2, one, vals))
    out = jnp.where(head_a, acc_a, acc_b)
    den = pltpu.roll(jnp.where(head_a, acc_b, acc_a), HEAD_DIM, 1)
    mx = jnp.where(head_a, m[:blk], m[blk:])
    return out / den, mx + jnp.log(den)


def _attn_kernel(q_ref, kp_ref, kc_ref, vp_ref, vc_ref, y_ref, o_scr, l_scr):
    blk = ATTN_BLOCK
    has_prev = pl.program_id(2) > 0
    head_a = lax.broadcasted_iota(jnp.int32, (blk, LANES), 1) < HEAD_DIM
    row = lax.broadcasted_iota(jnp.int32, (2 * blk, 2 * blk), 0) & (blk - 1)
    col = lax.broadcasted_iota(jnp.int32, (2 * blk, 2 * blk), 1)
    in_prev = col < blk
    col = col & (blk - 1)

    def window_mask(index_of_row, prev_offset):
        delta = index_of_row(row) - index_of_row(col)
        delta = jnp.where(in_prev, delta + prev_offset, delta)
        return jnp.logical_and(delta >= 0, delta <= blk)

    def gather(ref, planes, lo, size):
        parts = [ref[0, 0, r, lo:lo + size, :] for r in planes]
        return parts[0] if len(parts) == 1 else jnp.concatenate(parts, axis=0)

    for pat, dilation in enumerate((16, 4, 1)):
        per_plane = blk * dilation // PLANES
        n_groups = dilation
        if dilation == 16:
            index_of_row = lambda i: i
        elif dilation == 4:
            index_of_row = lambda i: ((i & 31) << 2) + (i >> 5)
        else:
            index_of_row = lambda i: ((i & 7) << 4) + (i >> 3)
        mask_mid = window_mask(index_of_row, blk)
        mask_first = window_mask(index_of_row, jnp.where(has_prev, blk, 4 * blk))
        for res in range(n_groups):
            planes = list(range(res, PLANES, dilation))
            for n in range(blk // per_plane):
                lo = n * per_plane
                q = gather(q_ref, planes, lo, per_plane)
                k_cur = gather(kc_ref, planes, lo, per_plane)
                v_cur = gather(vc_ref, planes, lo, per_plane)
                if n == 0:
                    k_prev = gather(kp_ref, planes, blk - per_plane, per_plane)
                    v_prev = gather(vp_ref, planes, blk - per_plane, per_plane)
                    mask = mask_first
                else:
                    k_prev = gather(kc_ref, planes, lo - per_plane, per_plane)
                    v_prev = gather(vc_ref, planes, lo - per_plane, per_plane)
                    mask = mask_mid
                out, lse = _attend_block(q, k_prev, k_cur, v_prev, v_cur, mask, head_a)
                for j, r in enumerate(planes):
                    o_scr[pat, r, lo:lo + per_plane, :] = out[j * per_plane:(j + 1) * per_plane]
                    l_scr[pat, r, lo:lo + per_plane, :] = lse[j * per_plane:(j + 1) * per_plane]

    for r in range(PLANES):
        la, lb, lc = l_scr[0, r], l_scr[1, r], l_scr[2, r]
        m = jnp.maximum(jnp.maximum(la, lb), lc)
        ea, eb, ec = jnp.exp(la - m), jnp.exp(lb - m), jnp.exp(lc - m)
        y = (ea * o_scr[0, r] + eb * o_scr[1, r] + ec * o_scr[2, r]) / (ea + eb + ec)
        y_ref[0, 0, pl.ds(r, blk, stride=PLANES), :] = y


def _dilated_attention(q, k, v):
    batch, n_pair, planes, rows, lanes = q.shape
    blk = ATTN_BLOCK
    tile = planes * blk
    cur = pl.BlockSpec((1, 1, planes, blk, lanes), lambda b, h, t: (b, h, 0, t, 0))
    prev = pl.BlockSpec((1, 1, planes, blk, lanes), lambda b, h, t: (b, h, 0, jnp.maximum(t - 1, 0), 0))
    return pl.pallas_call(
        _attn_kernel,
        grid=(batch, n_pair, rows // blk),
        in_specs=[cur, prev, cur, prev, cur],
        out_specs=pl.BlockSpec((1, 1, tile, lanes), lambda b, h, t: (b, h, t, 0)),
        out_shape=jax.ShapeDtypeStruct((batch, n_pair, planes * rows, lanes), F32),
        scratch_shapes=[pltpu.VMEM((len(DILATED_PAIRS), planes, blk, lanes), F32)] * 2,
        compiler_params=pltpu.CompilerParams(
            dimension_semantics=("arbitrary",) * 3, vmem_limit_bytes=VMEM_LIMIT),
        name="dilated_attn",
    )(q, k, k, v, v)


def _hgrn_block(h, base, lb, nw_ref, hg_ref, y_ref, st_ref, b_scr, q_scr, k_scr, a_scr, tri, pair_key):
    t_blk = HGRN_BLOCK
    dk = HGRN_DK
    sub = HGRN_SUB
    w = HGRN_WIDTH
    rows = pl.ds(base, t_blk)
    sl = slice(h * dk, (h + 1) * dk)
    qr = hg_ref[0, rows, h * dk:(h + 1) * dk]
    fl = hg_ref[0, rows, w + h * dk:w + (h + 1) * dk]
    vi = hg_ref[0, rows, 2 * w + h * dk:2 * w + (h + 1) * dk]
    gr = hg_ref[0, rows, 3 * w + h * dk:3 * w + (h + 1) * dk]
    f = lb + (1.0 - lb) * _sigmoid(fl)
    lf = jnp.log2(f)
    key = 1.0 - f
    qf = qr * _sigmoid(qr)
    v_bf = vi.astype(BF16)

    hi = lf.astype(BF16)
    r1 = lf - hi.astype(F32)
    mid = r1.astype(BF16)
    lo = (r1 - mid.astype(F32)).astype(BF16)
    b = _dot(tri, hi) + _dot(tri, mid) + _dot(tri, lo)
    b_scr[...] = b
    q_scr[...] = qf
    k_scr[...] = jnp.log2(key) - b

    b_last = b_scr[t_blk - 1:t_blk, :]
    st = st_ref[h]
    q_inter = (qf * jnp.exp2(b)).astype(BF16)
    o = _dot_nt(q_inter, st.astype(BF16))
    k_upd = (key * jnp.exp2(b_last - b)).astype(BF16)
    st_ref[h] = st * jnp.exp2(b_last) + _dot_tn(v_bf, k_upd)

    sub_id = lax.broadcasted_iota(jnp.int32, (sub, dk), 0)
    lane_id = lax.broadcasted_iota(jnp.int32, (sub, t_blk), 1)
    for c in range(t_blk // sub):
        lo_row = c * sub
        b_t = b_scr[lo_row:lo_row + sub, :]
        q_t = q_scr[lo_row:lo_row + sub, :]
        blk_rows = jnp.zeros((sub, t_blk), F32)
        for s in range(sub):
            c_s = k_scr[lo_row + s:lo_row + s + 1, :]
            dec_k = jnp.exp2(jnp.where(sub_id >= s, b_t + c_s, -jnp.inf))
            col = jnp.sum(q_t * dec_k, axis=-1, keepdims=True)
            blk_rows = jnp.where(lane_id == lo_row + s, col, blk_rows)
        a_scr[lo_row:lo_row + sub, :] = blk_rows

    a_mat = a_scr[...]
    rid = lax.broadcasted_iota(jnp.int32, (t_blk, dk), 0)
    half = sub
    while half < t_blk:
        span = 2 * half
        refs = [jnp.broadcast_to(b_scr[p * span + half - 1:p * span + half, :], (span, dk))
                for p in range(t_blk // span)]
        d = b - (refs[0] if len(refs) == 1 else jnp.concatenate(refs, axis=0))
        upper = (rid & half) != 0
        x = (jnp.where(upper, qf, key) * jnp.exp2(jnp.where(upper, d, -d))).astype(BF16)
        a_mat = jnp.where(pair_key == half, _dot_nt(x, x), a_mat)
        half = span

    o = o + _dot(a_mat.astype(BF16), v_bf)
    o = _rms_scale(o) * nw_ref[:, sl]
    y_ref[0, rows, sl] = (o * (gr * _sigmoid(gr))).astype(BF16)


def _hgrn_kernel(lbl_ref, nw_ref, hg_ref, y_ref, st_ref, b_scr, q_scr, k_scr, a_scr):
    t_blk = HGRN_BLOCK

    @pl.when(pl.program_id(1) == 0)
    def _():
        st_ref[...] = jnp.zeros_like(st_ref)

    lbl = lbl_ref[...]
    mx = jnp.max(lbl, axis=0, keepdims=True)
    e = jnp.exp(lbl - mx)
    lb_all = e[0:1, :] / jnp.sum(e, axis=0, keepdims=True)

    row = lax.broadcasted_iota(jnp.int32, (t_blk, t_blk), 0)
    col = lax.broadcasted_iota(jnp.int32, (t_blk, t_blk), 1)
    tri = (col <= row).astype(BF16)
    pair_key = jnp.zeros((t_blk, t_blk), jnp.int32)
    half = HGRN_SUB
    while half < t_blk:
        pair_key = jnp.where(jnp.logical_and((row ^ col) >= half, col < row), half, pair_key)
        half *= 2

    def body(i, carry):
        base = pl.multiple_of(i * t_blk, t_blk)
        for h in range(HGRN_HEADS):
            lb = lb_all[:, h * HGRN_DK:(h + 1) * HGRN_DK]
            _hgrn_block(h, base, lb, nw_ref, hg_ref, y_ref, st_ref, b_scr, q_scr, k_scr, a_scr,
                        tri, pair_key)
        return carry

    lax.fori_loop(0, hg_ref.shape[1] // t_blk, body, 0)


def _hgrn2(hg3, lb_logits, norm_w, rows_per_step=512):
    batch, seq, _ = hg3.shape
    blk_scratch = pltpu.VMEM((HGRN_BLOCK, HGRN_DK), F32)
    return pl.pallas_call(
        _hgrn_kernel,
        grid=(batch, seq // rows_per_step),
        in_specs=[_resident(lb_logits.shape), _resident((1, HGRN_WIDTH)),
                  pl.BlockSpec((1, rows_per_step, 4 * HGRN_WIDTH), lambda b, t: (b, t, 0))],
        out_specs=pl.BlockSpec((1, rows_per_step, HGRN_WIDTH), lambda b, t: (b, t, 0)),
        out_shape=jax.ShapeDtypeStruct((batch, seq, HGRN_WIDTH), BF16),
        scratch_shapes=[pltpu.VMEM((HGRN_HEADS, HGRN_DK, HGRN_DK), F32),
                        blk_scratch, blk_scratch, blk_scratch,
                        pltpu.VMEM((HGRN_BLOCK, HGRN_BLOCK), F32)],
        compiler_params=pltpu.CompilerParams(
            dimension_semantics=("arbitrary", "arbitrary"), vmem_limit_bytes=VMEM_LIMIT),
        name="hgrn2",
    )(lb_logits, norm_w, hg3)


def _out_ffn_kernel(ya_ref, yb_ref, x_ref, wo_ref, n2_ref, wgu_ref, wd_ref, nf_ref, out_ref, *, chunk):
    h = x_ref[0] + _dot(yb_ref[0], wo_ref[0, ATTN_WIDTH:, :].astype(BF16))
    for j in range(ATTN_WIDTH // LANES):
        h = h + _dot(ya_ref[0, j].astype(BF16), wo_ref[0, j * LANES:(j + 1) * LANES, :].astype(BF16))
    u = (_rms_scale(h) * n2_ref[...]).astype(BF16)
    ffn = jnp.zeros_like(h)
    for c in range(FFN_HIDDEN // chunk):
        cols = slice(c * chunk, (c + 1) * chunk)
        up_cols = slice(FFN_HIDDEN + c * chunk, FFN_HIDDEN + (c + 1) * chunk)
        gate = _dot(u, wgu_ref[0, :, cols].astype(BF16))
        up = _dot(u, wgu_ref[0, :, up_cols].astype(BF16))
        act = (gate * _sigmoid(gate) * up).astype(BF16)
        ffn = ffn + _dot(act, wd_ref[0, cols, :].astype(BF16))
    out_ref[0] = _rms_scale(h + ffn) * nf_ref[...]


def _out_ffn(ya, yb, x, w_out, norm2_w, w_gate_up, w_down, final_w, tm=512, chunk=256):
    batch, seq, _ = x.shape
    n_pair = ATTN_WIDTH // LANES
    row_spec = lambda width: pl.BlockSpec((1, tm, width), lambda b, i: (b, i, 0))
    return pl.pallas_call(
        functools.partial(_out_ffn_kernel, chunk=chunk),
        grid=(batch, seq // tm),
        in_specs=[pl.BlockSpec((1, n_pair, tm, LANES), lambda b, i: (b, 0, i, 0)),
                  row_spec(HGRN_WIDTH), row_spec(D_MODEL),
                  _resident(w_out.shape), _resident((1, D_MODEL)), _resident(w_gate_up.shape),
                  _resident(w_down.shape), _resident((1, D_MODEL))],
        out_specs=row_spec(D_MODEL),
        out_shape=jax.ShapeDtypeStruct((batch, seq, D_MODEL), F32),
        compiler_params=pltpu.CompilerParams(
            dimension_semantics=("arbitrary", "arbitrary"), vmem_limit_bytes=VMEM_LIMIT),
        name="out_proj_ffn",
    )(ya, yb, x, w_out, norm2_w, w_gate_up, w_down, final_w)


def kernel(x, norm1_w, w_in, lb_logits, hgrn_norm_w, w_out, norm2_w, w_gate_up, w_down, final_norm_w):
    batch, seq, d_model = x.shape
    assert d_model == D_MODEL and norm1_w.shape[0] == 1
    assert all(w // d == ATTN_BLOCK for w, d in DILATED_PAIRS)
    assert tuple(d for _, d in DILATED_PAIRS) == (1, 4, 16) and seq % (PLANES * ATTN_BLOCK) == 0
    cos_t, sin_t = _rope_tables(seq)
    q, k, v, hg = _in_proj(x, norm1_w, w_in, cos_t, sin_t)
    ya = _dilated_attention(q, k, v)
    yb = _hgrn2(hg, lb_logits, hgrn_norm_w)
    return _out_ffn(ya, yb, x, w_out, norm2_w, w_gate_up, w_down, final_norm_w.reshape(1, d_model))
```

```python
import functools

import numpy as np
import jax
import jax.numpy as jnp
from jax import lax
from jax.experimental import pallas as pl
from jax.experimental.pallas import tpu as pltpu

D_MODEL = 1024
ATTN_WIDTH = 512
ATTN_HEADS = 8
HEAD_DIM = 64
DILATED_PAIRS = ((128, 1), (512, 4), (2048, 16))
ATTN_BLOCK = 128
ROPE_THETA = 10000.0
HGRN_WIDTH = 512
HGRN_DK = 128
HGRN_HEADS = 4
FFN_HIDDEN = 2816
IN_PROJ_WIDTH = 3 * ATTN_WIDTH + 4 * HGRN_WIDTH
NORM_EPS = 1e-6

LANES = 128
PLANES = 16
ATTN_SKEW_SCORES = 2
ATTN_SKEW_FINISH = 2
HGRN_BLOCK = 128
HGRN_SUB = 8
VMEM_LIMIT = 56 * 1024 * 1024

F32 = jnp.float32
BF16 = jnp.bfloat16


def _dot(a, b):
    return jnp.dot(a, b, preferred_element_type=F32)


def _dot_nt(a, b):
    return lax.dot_general(a, b, (((1,), (1,)), ((), ())), preferred_element_type=F32)


def _dot_tn(a, b):
    return lax.dot_general(a, b, (((0,), (0,)), ((), ())), preferred_element_type=F32)


def _rms_scale(x):
    return x * lax.rsqrt(jnp.mean(x * x, axis=-1, keepdims=True) + NORM_EPS)


def _sigmoid(x):
    return 1.0 / (1.0 + jnp.exp(-x))


def _resident(shape):
    return pl.BlockSpec(shape, lambda *_: (0,) * len(shape), pipeline_mode=pl.Buffered(1))


def _rope_tables(seq):
    half = HEAD_DIM // 2
    inv_freq = ROPE_THETA ** (-np.arange(half, dtype=np.float64) / half)
    ang = np.arange(seq, dtype=np.float64)[:, None] * inv_freq[None, :]
    cos, sin = np.cos(ang), np.sin(ang)
    cos_t = np.concatenate([cos, cos, cos, cos], axis=1)
    sin_t = np.concatenate([-sin, sin, -sin, sin], axis=1)
    return jnp.asarray(cos_t, F32), jnp.asarray(sin_t, F32)


def _rotate_half_pairs(t, cos, sin):
    lane = lax.broadcasted_iota(jnp.int32, t.shape, 1)
    fwd = pltpu.roll(t, LANES - HEAD_DIM // 2, 1)
    bwd = pltpu.roll(t, HEAD_DIM // 2, 1)
    partner = jnp.where((lane & (HEAD_DIM // 2)) == 0, fwd, bwd)
    return t * cos + partner * sin


def _in_proj_kernel(x_ref, nw_ref, w_ref, cos_ref, sin_ref, q_ref, k_ref, v_ref, hg_ref, stage_ref):
    u = (_rms_scale(x_ref[0]) * nw_ref[...]).astype(BF16)
    cos = cos_ref[...]
    sin = sin_ref[...]
    a = ATTN_WIDTH
    n_pair = a // LANES
    pq = _dot(u, w_ref[0, :, 0:a].astype(BF16))
    pk = _dot(u, w_ref[0, :, a:2 * a].astype(BF16))
    pv = _dot(u, w_ref[0, :, 2 * a:3 * a].astype(BF16))
    scale = HEAD_DIM ** -0.5 * float(np.log2(np.e))
    for j in range(n_pair):
        sl = slice(j * LANES, (j + 1) * LANES)
        stage_ref[j] = _rotate_half_pairs(pq[:, sl], cos, sin) * scale
        stage_ref[n_pair + j] = _rotate_half_pairs(pk[:, sl], cos, sin)
        stage_ref[2 * n_pair + j] = pv[:, sl]
    hg_ref[0] = _dot(u, w_ref[0, :, 3 * a:].astype(BF16))
    rows_per_plane = stage_ref.shape[1] // PLANES
    for i, out_ref in enumerate((q_ref, k_ref, v_ref)):
        for j in range(n_pair):
            for r in range(PLANES):
                out_ref[0, j, r] = stage_ref[i * n_pair + j, pl.ds(r, rows_per_plane, stride=PLANES), :]


def _in_proj(x, norm_w, w_in, cos_t, sin_t, tm=512):
    batch, seq, _ = x.shape
    n_pair = ATTN_WIDTH // LANES
    row_spec = lambda width: pl.BlockSpec((1, tm, width), lambda b, i: (b, i, 0))
    tab_spec = pl.BlockSpec((tm, LANES), lambda b, i: (i, 0))
    plane_spec = pl.BlockSpec((1, n_pair, PLANES, tm // PLANES, LANES), lambda b, i: (b, 0, 0, i, 0))
    plane_shape = jax.ShapeDtypeStruct((batch, n_pair, PLANES, seq // PLANES, LANES), F32)
    return pl.pallas_call(
        _in_proj_kernel,
        grid=(batch, seq // tm),
        in_specs=[row_spec(D_MODEL), _resident((1, D_MODEL)),
                  _resident(w_in.shape), tab_spec, tab_spec],
        out_specs=[plane_spec] * 3 + [row_spec(4 * HGRN_WIDTH)],
        out_shape=[plane_shape] * 3 + [jax.ShapeDtypeStruct((batch, seq, 4 * HGRN_WIDTH), F32)],
        scratch_shapes=[pltpu.VMEM((3 * n_pair, tm, LANES), F32)],
        compiler_params=pltpu.CompilerParams(
            dimension_semantics=("arbitrary", "arbitrary"), vmem_limit_bytes=VMEM_LIMIT),
        name="in_proj",
    )(x, norm_w, w_in, cos_t, sin_t)


def _block_scores(q, keys, head_a):
    qb = q.astype(BF16)
    zero = jnp.zeros_like(qb)
    lhs = jnp.concatenate([jnp.where(head_a, qb, zero), jnp.where(head_a, zero, qb)], axis=0)
    return _dot_nt(lhs, keys)


def _block_softmax_pv(s, bias, vals_a, vals_b):
    blk = ATTN_BLOCK
    accs, maxes = [], []
    for half, vals in enumerate((vals_a, vals_b)):
        sh = s[half * blk:(half + 1) * blk] + bias
        m = jnp.max(sh, axis=-1, keepdims=True)
        accs.append(_dot(jnp.exp2(sh - m).astype(BF16), vals))
        maxes.append(m)
    return accs, maxes


def _block_finish(accs, maxes, head_a):
    out = jnp.where(head_a, accs[0], accs[1])
    den = pltpu.roll(jnp.where(head_a, accs[1], accs[0]), HEAD_DIM, 1)
    mx = jnp.where(head_a, maxes[0], maxes[1])
    return out, den, mx


def _block_row_index(dilation):
    per_plane = ATTN_BLOCK * dilation // PLANES
    shift = per_plane.bit_length() - 1
    return lambda i: ((i & (per_plane - 1)) * (PLANES // dilation)) + (i >> shift)


def _attn_kernel(q_ref, kp_ref, kc_ref, vp_ref, vc_ref, y_ref, o_scr, d_scr, m_scr, bias_scr):
    blk = ATTN_BLOCK
    dilations = (16, 4, 1)
    head_a = lax.broadcasted_iota(jnp.int32, (blk, LANES), 1) < HEAD_DIM

    @pl.when((pl.program_id(0) == 0) & (pl.program_id(1) == 0) & (pl.program_id(2) == 0))
    def _():
        row = lax.broadcasted_iota(jnp.int32, (blk, 2 * blk), 0)
        col = lax.broadcasted_iota(jnp.int32, (blk, 2 * blk), 1)
        in_prev = col < blk
        col = col & (blk - 1)
        for pat, dilation in enumerate(dilations):
            index_of_row = _block_row_index(dilation)
            delta = index_of_row(row) - index_of_row(col)
            for variant, prev_offset in enumerate((4 * blk, blk)):
                d = jnp.where(in_prev, delta + prev_offset, delta)
                bias_scr[variant, pat] = jnp.where(jnp.logical_and(d >= 0, d <= blk), 0.0, -jnp.inf)

    has_prev = (pl.program_id(2) > 0).astype(jnp.int32)

    def gather(ref, planes, lo, size):
        parts = [ref[0, 0, r, lo:lo + size, :] for r in planes]
        return parts[0] if len(parts) == 1 else jnp.concatenate(parts, axis=0)

    def keys_values(k_ref, v_ref, planes, lo, size):
        k = gather(k_ref, planes, lo, size).astype(BF16)
        v = gather(v_ref, planes, lo, size).astype(BF16)
        one = jnp.ones_like(v)
        return k, jnp.where(head_a, v, one), jnp.where(head_a, one, v)

    blocks = []
    for pat, dilation in enumerate(dilations):
        per_plane = blk * dilation // PLANES
        for res in range(dilation):
            planes = list(range(res, PLANES, dilation))
            blocks += [(pat, planes, per_plane, n) for n in range(blk // per_plane)]

    carried = {}

    def scores_stage(pat, planes, per_plane, n):
        lo = n * per_plane
        prev = keys_values(kp_ref, vp_ref, planes, blk - per_plane, per_plane) if n == 0 else carried["kv"]
        cur = carried["kv"] = keys_values(kc_ref, vc_ref, planes, lo, per_plane)
        keys, vals_a, vals_b = (jnp.concatenate([p, c], axis=0) for p, c in zip(prev, cur))
        return _block_scores(gather(q_ref, planes, lo, per_plane), keys, head_a), vals_a, vals_b

    def softmax_stage(pat, planes, per_plane, n, s, vals_a, vals_b):
        bias = bias_scr[has_prev, pat] if n == 0 else bias_scr[1, pat]
        return _block_softmax_pv(s, bias, vals_a, vals_b)

    def finish_stage(pat, planes, per_plane, n, accs, maxes):
        lo = n * per_plane
        out, den, mx = _block_finish(accs, maxes, head_a)
        for j, r in enumerate(planes):
            rows = slice(j * per_plane, (j + 1) * per_plane)
            o_scr[pat, r, lo:lo + per_plane, :] = out[rows]
            d_scr[pat, r, lo:lo + per_plane, :] = den[rows]
            m_scr[pat, r, lo:lo + per_plane, :] = mx[rows]

    in_scores, in_softmax = {}, {}
    for step in range(len(blocks) + ATTN_SKEW_SCORES + ATTN_SKEW_FINISH):
        if step < len(blocks):
            in_scores[step] = scores_stage(*blocks[step])
        mid = step - ATTN_SKEW_SCORES
        if 0 <= mid < len(blocks):
            in_softmax[mid] = softmax_stage(*blocks[mid], *in_scores.pop(mid))
        last = mid - ATTN_SKEW_FINISH
        if 0 <= last < len(blocks):
            finish_stage(*blocks[last], *in_softmax.pop(last))

    for r in range(PLANES):
        m = jnp.maximum(jnp.maximum(m_scr[0, r], m_scr[1, r]), m_scr[2, r])
        num = den = None
        for pat in range(len(dilations)):
            w = jnp.exp2(m_scr[pat, r] - m)
            num = w * o_scr[pat, r] if num is None else num + w * o_scr[pat, r]
            den = w * d_scr[pat, r] if den is None else den + w * d_scr[pat, r]
        y_ref[0, 0, pl.ds(r, blk, stride=PLANES), :] = num / den


def _dilated_attention(q, k, v):
    batch, n_pair, planes, rows, lanes = q.shape
    blk = ATTN_BLOCK
    tile = planes * blk
    cur = pl.BlockSpec((1, 1, planes, blk, lanes), lambda b, h, t: (b, h, 0, t, 0))
    prev = pl.BlockSpec((1, 1, planes, blk, lanes), lambda b, h, t: (b, h, 0, jnp.maximum(t - 1, 0), 0))
    return pl.pallas_call(
        _attn_kernel,
        grid=(batch, n_pair, rows // blk),
        in_specs=[cur, prev, cur, prev, cur],
        out_specs=pl.BlockSpec((1, 1, tile, lanes), lambda b, h, t: (b, h, t, 0)),
        out_shape=jax.ShapeDtypeStruct((batch, n_pair, planes * rows, lanes), F32),
        scratch_shapes=[pltpu.VMEM((len(DILATED_PAIRS), planes, blk, lanes), F32)] * 3
        + [pltpu.VMEM((2, len(DILATED_PAIRS), blk, 2 * blk), F32)],
        compiler_params=pltpu.CompilerParams(
            dimension_semantics=("arbitrary",) * 3, vmem_limit_bytes=VMEM_LIMIT),
        name="dilated_attn",
    )(q, k, k, v, v)


def _hgrn_block(h, base, lb, nw_ref, hg_ref, y_ref, st_ref, b_scr, q_scr, k_scr, a_scr, tri, pair_key):
    t_blk = HGRN_BLOCK
    dk = HGRN_DK
    sub = HGRN_SUB
    w = HGRN_WIDTH
    rows = pl.ds(base, t_blk)
    sl = slice(h * dk, (h + 1) * dk)
    qr = hg_ref[0, rows, h * dk:(h + 1) * dk]
    fl = hg_ref[0, rows, w + h * dk:w + (h + 1) * dk]
    vi = hg_ref[0, rows, 2 * w + h * dk:2 * w + (h + 1) * dk]
    gr = hg_ref[0, rows, 3 * w + h * dk:3 * w + (h + 1) * dk]
    f = lb + (1.0 - lb) * _sigmoid(fl)
    lf = jnp.log2(f)
    key = 1.0 - f
    qf = qr * _sigmoid(qr)
    v_bf = vi.astype(BF16)

    hi = lf.astype(BF16)
    r1 = lf - hi.astype(F32)
    mid = r1.astype(BF16)
    lo = (r1 - mid.astype(F32)).astype(BF16)
    b = _dot(tri, hi) + _dot(tri, mid) + _dot(tri, lo)
    b_scr[...] = b
    q_scr[...] = qf
    k_scr[...] = jnp.log2(key) - b

    b_last = b_scr[t_blk - 1:t_blk, :]
    st = st_ref[h]
    q_inter = (qf * jnp.exp2(b)).astype(BF16)
    o = _dot_nt(q_inter, st.astype(BF16))
    k_upd = (key * jnp.exp2(b_last - b)).astype(BF16)
    st_ref[h] = st * jnp.exp2(b_last) + _dot_tn(v_bf, k_upd)

    sub_id = lax.broadcasted_iota(jnp.int32, (sub, dk), 0)
    lane_id = lax.broadcasted_iota(jnp.int32, (sub, t_blk), 1)
    for c in range(t_blk // sub):
        lo_row = c * sub
        b_t = b_scr[lo_row:lo_row + sub, :]
        q_t = q_scr[lo_row:lo_row + sub, :]
        blk_rows = jnp.zeros((sub, t_blk), F32)
        for s in range(sub):
            c_s = k_scr[lo_row + s:lo_row + s + 1, :]
            dec_k = jnp.exp2(jnp.where(sub_id >= s, b_t + c_s, -jnp.inf))
            col = jnp.sum(q_t * dec_k, axis=-1, keepdims=True)
            blk_rows = jnp.where(lane_id == lo_row + s, col, blk_rows)
        a_scr[lo_row:lo_row + sub, :] = blk_rows

    a_mat = a_scr[...]
    rid = lax.broadcasted_iota(jnp.int32, (t_blk, dk), 0)
    half = sub
    while half < t_blk:
        span = 2 * half
        refs = [jnp.broadcast_to(b_scr[p * span + half - 1:p * span + half, :], (span, dk))
                for p in range(t_blk // span)]
        d = b - (refs[0] if len(refs) == 1 else jnp.concatenate(refs, axis=0))
        upper = (rid & half) != 0
        x = (jnp.where(upper, qf, key) * jnp.exp2(jnp.where(upper, d, -d))).astype(BF16)
        a_mat = jnp.where(pair_key == half, _dot_nt(x, x), a_mat)
        half = span

    o = o + _dot(a_mat.astype(BF16), v_bf)
    o = _rms_scale(o) * nw_ref[:, sl]
    y_ref[0, rows, sl] = (o * (gr * _sigmoid(gr))).astype(BF16)


def _hgrn_kernel(lbl_ref, nw_ref, hg_ref, y_ref, st_ref, b_scr, q_scr, k_scr, a_scr):
    t_blk = HGRN_BLOCK

    @pl.when(pl.program_id(1) == 0)
    def _():
        st_ref[...] = jnp.zeros_like(st_ref)

    lbl = lbl_ref[...]
    mx = jnp.max(lbl, axis=0, keepdims=True)
    e = jnp.exp(lbl - mx)
    lb_all = e[0:1, :] / jnp.sum(e, axis=0, keepdims=True)

    row = lax.broadcasted_iota(jnp.int32, (t_blk, t_blk), 0)
    col = lax.broadcasted_iota(jnp.int32, (t_blk, t_blk), 1)
    tri = (col <= row).astype(BF16)
    pair_key = jnp.zeros((t_blk, t_blk), jnp.int32)
    half = HGRN_SUB
    while half < t_blk:
        pair_key = jnp.where(jnp.logical_and((row ^ col) >= half, col < row), half, pair_key)
        half *= 2

    def body(i, carry):
        base = pl.multiple_of(i * t_blk, t_blk)
        for h in range(HGRN_HEADS):
            lb = lb_all[:, h * HGRN_DK:(h + 1) * HGRN_DK]
            _hgrn_block(h, base, lb, nw_ref, hg_ref, y_ref, st_ref, b_scr, q_scr, k_scr, a_scr,
                        tri, pair_key)
        return carry

    lax.fori_loop(0, hg_ref.shape[1] // t_blk, body, 0)


def _hgrn2(hg3, lb_logits, norm_w, rows_per_step=512):
    batch, seq, _ = hg3.shape
    blk_scratch = pltpu.VMEM((HGRN_BLOCK, HGRN_DK), F32)
    return pl.pallas_call(
        _hgrn_kernel,
        grid=(batch, seq // rows_per_step),
        in_specs=[_resident(lb_logits.shape), _resident((1, HGRN_WIDTH)),
                  pl.BlockSpec((1, rows_per_step, 4 * HGRN_WIDTH), lambda b, t: (b, t, 0))],
        out_specs=pl.BlockSpec((1, rows_per_step, HGRN_WIDTH), lambda b, t: (b, t, 0)),
        out_shape=jax.ShapeDtypeStruct((batch, seq, HGRN_WIDTH), BF16),
        scratch_shapes=[pltpu.VMEM((HGRN_HEADS, HGRN_DK, HGRN_DK), F32),
                        blk_scratch, blk_scratch, blk_scratch,
                        pltpu.VMEM((HGRN_BLOCK, HGRN_BLOCK), F32)],
        compiler_params=pltpu.CompilerParams(
            dimension_semantics=("arbitrary", "arbitrary"), vmem_limit_bytes=VMEM_LIMIT),
        name="hgrn2",
    )(lb_logits, norm_w, hg3)


def _out_ffn_kernel(ya_ref, yb_ref, x_ref, wo_ref, n2_ref, wgu_ref, wd_ref, nf_ref, out_ref, *, chunk):
    h = x_ref[0] + _dot(yb_ref[0], wo_ref[0, ATTN_WIDTH:, :].astype(BF16))
    for j in range(ATTN_WIDTH // LANES):
        h = h + _dot(ya_ref[0, j].astype(BF16), wo_ref[0, j * LANES:(j + 1) * LANES, :].astype(BF16))
    u = (_rms_scale(h) * n2_ref[...]).astype(BF16)
    ffn = jnp.zeros_like(h)
    for c in range(FFN_HIDDEN // chunk):
        cols = slice(c * chunk, (c + 1) * chunk)
        up_cols = slice(FFN_HIDDEN + c * chunk, FFN_HIDDEN + (c + 1) * chunk)
        gate = _dot(u, wgu_ref[0, :, cols].astype(BF16))
        up = _dot(u, wgu_ref[0, :, up_cols].astype(BF16))
        act = (gate * _sigmoid(gate) * up).astype(BF16)
        ffn = ffn + _dot(act, wd_ref[0, cols, :].astype(BF16))
    out_ref[0] = _rms_scale(h + ffn) * nf_ref[...]


def _out_ffn(ya, yb, x, w_out, norm2_w, w_gate_up, w_down, final_w, tm=512, chunk=256):
    batch, seq, _ = x.shape
    n_pair = ATTN_WIDTH // LANES
    row_spec = lambda width: pl.BlockSpec((1, tm, width), lambda b, i: (b, i, 0))
    return pl.pallas_call(
        functools.partial(_out_ffn_kernel, chunk=chunk),
        grid=(batch, seq // tm),
        in_specs=[pl.BlockSpec((1, n_pair, tm, LANES), lambda b, i: (b, 0, i, 0)),
                  row_spec(HGRN_WIDTH), row_spec(D_MODEL),
                  _resident(w_out.shape), _resident((1, D_MODEL)), _resident(w_gate_up.shape),
                  _resident(w_down.shape), _resident((1, D_MODEL))],
        out_specs=row_spec(D_MODEL),
        out_shape=jax.ShapeDtypeStruct((batch, seq, D_MODEL), F32),
        compiler_params=pltpu.CompilerParams(
            dimension_semantics=("arbitrary", "arbitrary"), vmem_limit_bytes=VMEM_LIMIT),
        name="out_proj_ffn",
    )(ya, yb, x, w_out, norm2_w, w_gate_up, w_down, final_w)


def kernel(x, norm1_w, w_in, lb_logits, hgrn_norm_w, w_out, norm2_w, w_gate_up, w_down, final_norm_w):
    batch, seq, d_model = x.shape
    assert d_model == D_MODEL and norm1_w.shape[0] == 1
    assert all(w // d == ATTN_BLOCK for w, d in DILATED_PAIRS)
    assert tuple(d for _, d in DILATED_PAIRS) == (1, 4, 16) and seq % (PLANES * ATTN_BLOCK) == 0
    cos_t, sin_t = _rope_tables(seq)
    q, k, v, hg = _in_proj(x, norm1_w, w_in, cos_t, sin_t)
    ya = _dilated_attention(q, k, v)
    yb = _hgrn2(hg, lb_logits, hgrn_norm_w)
    return _out_ffn(ya, yb, x, w_out, norm2_w, w_gate_up, w_down, final_norm_w.reshape(1, d_model))
```

```python
import functools

import numpy as np
import jax
import jax.numpy as jnp
from jax import lax
from jax.experimental import pallas as pl
from jax.experimental.pallas import tpu as pltpu

D_MODEL = 1024
ATTN_WIDTH = 512
ATTN_HEADS = 8
HEAD_DIM = 64
DILATED_PAIRS = ((128, 1), (512, 4), (2048, 16))
ATTN_BLOCK = 128
ROPE_THETA = 10000.0
HGRN_WIDTH = 512
HGRN_DK = 128
HGRN_HEADS = 4
FFN_HIDDEN = 2816
IN_PROJ_WIDTH = 3 * ATTN_WIDTH + 4 * HGRN_WIDTH
NORM_EPS = 1e-6

LANES = 128
PLANES = 16
ATTN_SKEW_SCORES = 2
ATTN_SKEW_FINISH = 2
HGRN_BLOCK = 128
HGRN_SUB = 8
VMEM_LIMIT = 56 * 1024 * 1024

F32 = jnp.float32
BF16 = jnp.bfloat16


def _dot(a, b):
    return jnp.dot(a, b, preferred_element_type=F32)


def _dot_nt(a, b):
    return lax.dot_general(a, b, (((1,), (1,)), ((), ())), preferred_element_type=F32)


def _dot_tn(a, b):
    return lax.dot_general(a, b, (((0,), (0,)), ((), ())), preferred_element_type=F32)


def _rms_scale(x):
    return x * lax.rsqrt(jnp.mean(x * x, axis=-1, keepdims=True) + NORM_EPS)


def _sigmoid(x):
    return 1.0 / (1.0 + jnp.exp(-x))


def _resident(shape):
    return pl.BlockSpec(shape, lambda *_: (0,) * len(shape), pipeline_mode=pl.Buffered(1))


def _rope_tables(seq):
    half = HEAD_DIM // 2
    inv_freq = ROPE_THETA ** (-np.arange(half, dtype=np.float64) / half)
    ang = np.arange(seq, dtype=np.float64)[:, None] * inv_freq[None, :]
    cos, sin = np.cos(ang), np.sin(ang)
    cos_t = np.concatenate([cos, cos, cos, cos], axis=1)
    sin_t = np.concatenate([-sin, sin, -sin, sin], axis=1)
    return jnp.asarray(cos_t, F32), jnp.asarray(sin_t, F32)


def _rotate_half_pairs(t, cos, sin):
    lane = lax.broadcasted_iota(jnp.int32, t.shape, 1)
    fwd = pltpu.roll(t, LANES - HEAD_DIM // 2, 1)
    bwd = pltpu.roll(t, HEAD_DIM // 2, 1)
    partner = jnp.where((lane & (HEAD_DIM // 2)) == 0, fwd, bwd)
    return t * cos + partner * sin


def _in_proj_kernel(x_ref, nw_ref, w_ref, cos_ref, sin_ref, q_ref, k_ref, v_ref, hg_ref, stage_ref):
    u = (_rms_scale(x_ref[0]) * nw_ref[...]).astype(BF16)
    cos = cos_ref[...]
    sin = sin_ref[...]
    a = ATTN_WIDTH
    n_pair = a // LANES
    pq = _dot(u, w_ref[0, :, 0:a].astype(BF16))
    pk = _dot(u, w_ref[0, :, a:2 * a].astype(BF16))
    pv = _dot(u, w_ref[0, :, 2 * a:3 * a].astype(BF16))
    scale = HEAD_DIM ** -0.5 * float(np.log2(np.e))
    for j in range(n_pair):
        sl = slice(j * LANES, (j + 1) * LANES)
        stage_ref[j] = _rotate_half_pairs(pq[:, sl], cos, sin) * scale
        stage_ref[n_pair + j] = _rotate_half_pairs(pk[:, sl], cos, sin)
        stage_ref[2 * n_pair + j] = pv[:, sl]
    hg_ref[0] = _dot(u, w_ref[0, :, 3 * a:].astype(BF16))
    rows_per_plane = stage_ref.shape[1] // PLANES
    for i, out_ref in enumerate((q_ref, k_ref, v_ref)):
        for j in range(n_pair):
            for r in range(PLANES):
                out_ref[0, j, r] = stage_ref[i * n_pair + j, pl.ds(r, rows_per_plane, stride=PLANES), :]


def _in_proj(x, norm_w, w_in, cos_t, sin_t, tm=512):
    batch, seq, _ = x.shape
    n_pair = ATTN_WIDTH // LANES
    row_spec = lambda width: pl.BlockSpec((1, tm, width), lambda b, i: (b, i, 0))
    tab_spec = pl.BlockSpec((tm, LANES), lambda b, i: (i, 0))
    plane_spec = pl.BlockSpec((1, n_pair, PLANES, tm // PLANES, LANES), lambda b, i: (b, 0, 0, i, 0))
    plane_shape = jax.ShapeDtypeStruct((batch, n_pair, PLANES, seq // PLANES, LANES), F32)
    return pl.pallas_call(
        _in_proj_kernel,
        grid=(batch, seq // tm),
        in_specs=[row_spec(D_MODEL), _resident((1, D_MODEL)),
                  _resident(w_in.shape), tab_spec, tab_spec],
        out_specs=[plane_spec] * 3 + [row_spec(4 * HGRN_WIDTH)],
        out_shape=[plane_shape] * 3 + [jax.ShapeDtypeStruct((batch, seq, 4 * HGRN_WIDTH), F32)],
        scratch_shapes=[pltpu.VMEM((3 * n_pair, tm, LANES), F32)],
        compiler_params=pltpu.CompilerParams(
            dimension_semantics=("arbitrary", "arbitrary"), vmem_limit_bytes=VMEM_LIMIT),
        name="in_proj",
    )(x, norm_w, w_in, cos_t, sin_t)


def _block_scores(q, keys, head_a):
    qb = q.astype(BF16)
    zero = jnp.zeros_like(qb)
    lhs = jnp.concatenate([jnp.where(head_a, qb, zero), jnp.where(head_a, zero, qb)], axis=0)
    return _dot_nt(lhs, keys)


def _block_softmax_pv(s, bias, vals_a, vals_b):
    blk = ATTN_BLOCK
    accs, maxes = [], []
    for half, vals in enumerate((vals_a, vals_b)):
        sh = s[half * blk:(half + 1) * blk] + bias
        m = jnp.max(sh, axis=-1, keepdims=True)
        accs.append(_dot(jnp.exp2(sh - m).astype(BF16), vals))
        maxes.append(m)
    return accs, maxes


def _block_finish(accs, maxes, head_a):
    out = jnp.where(head_a, accs[0], accs[1])
    den = pltpu.roll(jnp.where(head_a, accs[1], accs[0]), HEAD_DIM, 1)
    mx = jnp.where(head_a, maxes[0], maxes[1])
    return out, den, mx


def _block_row_index(dilation):
    per_plane = ATTN_BLOCK * dilation // PLANES
    shift = per_plane.bit_length() - 1
    return lambda i: ((i & (per_plane - 1)) * (PLANES // dilation)) + (i >> shift)


def _attn_kernel(q_ref, kp_ref, kc_ref, vp_ref, vc_ref, y_ref, o_scr, d_scr, m_scr, bias_scr):
    blk = ATTN_BLOCK
    dilations = (16, 4, 1)
    head_a = lax.broadcasted_iota(jnp.int32, (blk, LANES), 1) < HEAD_DIM

    @pl.when((pl.program_id(0) == 0) & (pl.program_id(1) == 0) & (pl.program_id(2) == 0))
    def _():
        row = lax.broadcasted_iota(jnp.int32, (blk, 2 * blk), 0)
        col = lax.broadcasted_iota(jnp.int32, (blk, 2 * blk), 1)
        in_prev = col < blk
        col = col & (blk - 1)
        for pat, dilation in enumerate(dilations):
            index_of_row = _block_row_index(dilation)
            delta = index_of_row(row) - index_of_row(col)
            for variant, prev_offset in enumerate((4 * blk, blk)):
                d = jnp.where(in_prev, delta + prev_offset, delta)
                bias_scr[variant, pat] = jnp.where(jnp.logical_and(d >= 0, d <= blk), 0.0, -jnp.inf)

    has_prev = (pl.program_id(2) > 0).astype(jnp.int32)

    def gather(ref, planes, lo, size):
        parts = [ref[0, 0, r, lo:lo + size, :] for r in planes]
        return parts[0] if len(parts) == 1 else jnp.concatenate(parts, axis=0)

    def keys_values(k_ref, v_ref, planes, lo, size):
        k = gather(k_ref, planes, lo, size).astype(BF16)
        v = gather(v_ref, planes, lo, size).astype(BF16)
        one = jnp.ones_like(v)
        return k, jnp.where(head_a, v, one), jnp.where(head_a, one, v)

    blocks = []
    for pat, dilation in enumerate(dilations):
        per_plane = blk * dilation // PLANES
        for res in range(dilation):
            planes = list(range(res, PLANES, dilation))
            blocks += [(pat, planes, per_plane, n) for n in range(blk // per_plane)]

    carried = {}

    def scores_stage(pat, planes, per_plane, n):
        lo = n * per_plane
        prev = keys_values(kp_ref, vp_ref, planes, blk - per_plane, per_plane) if n == 0 else carried["kv"]
        cur = carried["kv"] = keys_values(kc_ref, vc_ref, planes, lo, per_plane)
        keys, vals_a, vals_b = (jnp.concatenate([p, c], axis=0) for p, c in zip(prev, cur))
        return _block_scores(gather(q_ref, planes, lo, per_plane), keys, head_a), vals_a, vals_b

    def softmax_stage(pat, planes, per_plane, n, s, vals_a, vals_b):
        bias = bias_scr[has_prev, pat] if n == 0 else bias_scr[1, pat]
        return _block_softmax_pv(s, bias, vals_a, vals_b)

    def finish_stage(pat, planes, per_plane, n, accs, maxes):
        lo = n * per_plane
        out, den, mx = _block_finish(accs, maxes, head_a)
        for j, r in enumerate(planes):
            rows = slice(j * per_plane, (j + 1) * per_plane)
            o_scr[pat, r, lo:lo + per_plane, :] = out[rows]
            d_scr[pat, r, lo:lo + per_plane, :] = den[rows]
            m_scr[pat, r, lo:lo + per_plane, :] = mx[rows]

    in_scores, in_softmax = {}, {}
    for step in range(len(blocks) + ATTN_SKEW_SCORES + ATTN_SKEW_FINISH):
        if step < len(blocks):
            in_scores[step] = scores_stage(*blocks[step])
        mid = step - ATTN_SKEW_SCORES
        if 0 <= mid < len(blocks):
            in_softmax[mid] = softmax_stage(*blocks[mid], *in_scores.pop(mid))
        last = mid - ATTN_SKEW_FINISH
        if 0 <= last < len(blocks):
            finish_stage(*blocks[last], *in_softmax.pop(last))

    for r in range(PLANES):
        m = jnp.maximum(jnp.maximum(m_scr[0, r], m_scr[1, r]), m_scr[2, r])
        num = den = None
        for pat in range(len(dilations)):
            w = jnp.exp2(m_scr[pat, r] - m)
            num = w * o_scr[pat, r] if num is None else num + w * o_scr[pat, r]
            den = w * d_scr[pat, r] if den is None else den + w * d_scr[pat, r]
        y_ref[0, 0, pl.ds(r, blk, stride=PLANES), :] = num / den


def _dilated_attention(q, k, v):
    batch, n_pair, planes, rows, lanes = q.shape
    blk = ATTN_BLOCK
    tile = planes * blk
    cur = pl.BlockSpec((1, 1, planes, blk, lanes), lambda b, h, t: (b, h, 0, t, 0))
    prev = pl.BlockSpec((1, 1, planes, blk, lanes), lambda b, h, t: (b, h, 0, jnp.maximum(t - 1, 0), 0))
    return pl.pallas_call(
        _attn_kernel,
        grid=(batch, n_pair, rows // blk),
        in_specs=[cur, prev, cur, prev, cur],
        out_specs=pl.BlockSpec((1, 1, tile, lanes), lambda b, h, t: (b, h, t, 0)),
        out_shape=jax.ShapeDtypeStruct((batch, n_pair, planes * rows, lanes), F32),
        scratch_shapes=[pltpu.VMEM((len(DILATED_PAIRS), planes, blk, lanes), F32)] * 3
        + [pltpu.VMEM((2, len(DILATED_PAIRS), blk, 2 * blk), F32)],
        compiler_params=pltpu.CompilerParams(
            dimension_semantics=("arbitrary",) * 3, vmem_limit_bytes=VMEM_LIMIT),
        name="dilated_attn",
    )(q, k, k, v, v)


def _hgrn_blocks(base, lb_all, nw_ref, hg_ref, y_ref, st_ref, b_scr, q_scr, k_scr, a_scr, tri, pair_key):
    t_blk = HGRN_BLOCK
    dk = HGRN_DK
    sub = HGRN_SUB
    w = HGRN_WIDTH
    rows = pl.ds(base, t_blk)
    heads = range(HGRN_HEADS)
    cols = [slice(h * dk, (h + 1) * dk) for h in heads]
    qf, key, v_bf, b, o = {}, {}, {}, {}, {}

    for h in heads:
        lb = lb_all[:, cols[h]]
        qr = hg_ref[0, rows, h * dk:(h + 1) * dk]
        fl = hg_ref[0, rows, w + h * dk:w + (h + 1) * dk]
        f = lb + (1.0 - lb) * _sigmoid(fl)
        lf = jnp.log2(f)
        key[h] = 1.0 - f
        qf[h] = qr * _sigmoid(qr)
        v_bf[h] = hg_ref[0, rows, 2 * w + h * dk:2 * w + (h + 1) * dk].astype(BF16)
        hi = lf.astype(BF16)
        r1 = lf - hi.astype(F32)
        mid = r1.astype(BF16)
        lo = (r1 - mid.astype(F32)).astype(BF16)
        b[h] = _dot(tri, hi) + _dot(tri, mid) + _dot(tri, lo)
        b_scr[h] = b[h]
        q_scr[h] = qf[h]
        k_scr[h] = jnp.log2(key[h]) - b[h]

    for h in heads:
        b_last = b_scr[h, t_blk - 1:t_blk, :]
        st = st_ref[h]
        q_inter = (qf[h] * jnp.exp2(b[h])).astype(BF16)
        o[h] = _dot_nt(q_inter, st.astype(BF16))
        k_upd = (key[h] * jnp.exp2(b_last - b[h])).astype(BF16)
        st_ref[h] = st * jnp.exp2(b_last) + _dot_tn(v_bf[h], k_upd)

    sub_id = lax.broadcasted_iota(jnp.int32, (sub, dk), 0)
    lane_id = lax.broadcasted_iota(jnp.int32, (sub, t_blk), 1)
    for c in range(t_blk // sub):
        lo_row = c * sub
        for h in heads:
            b_t = b_scr[h, lo_row:lo_row + sub, :]
            q_t = q_scr[h, lo_row:lo_row + sub, :]
            blk_rows = jnp.zeros((sub, t_blk), F32)
            for s in range(sub):
                c_s = k_scr[h, lo_row + s:lo_row + s + 1, :]
                dec_k = jnp.exp2(jnp.where(sub_id >= s, b_t + c_s, -jnp.inf))
                col = jnp.sum(q_t * dec_k, axis=-1, keepdims=True)
                blk_rows = jnp.where(lane_id == lo_row + s, col, blk_rows)
            a_scr[h, lo_row:lo_row + sub, :] = blk_rows

    a_mat = {h: a_scr[h] for h in heads}
    rid = lax.broadcasted_iota(jnp.int32, (t_blk, dk), 0)
    half = sub
    while half < t_blk:
        span = 2 * half
        upper = (rid & half) != 0
        for h in heads:
            refs = [jnp.broadcast_to(b_scr[h, p * span + half - 1:p * span + half, :], (span, dk))
                    for p in range(t_blk // span)]
            d = b[h] - (refs[0] if len(refs) == 1 else jnp.concatenate(refs, axis=0))
            x = (jnp.where(upper, qf[h], key[h]) * jnp.exp2(jnp.where(upper, d, -d))).astype(BF16)
            a_mat[h] = jnp.where(pair_key == half, _dot_nt(x, x), a_mat[h])
        half = span

    for h in heads:
        gr = hg_ref[0, rows, 3 * w + h * dk:3 * w + (h + 1) * dk]
        out = o[h] + _dot(a_mat[h].astype(BF16), v_bf[h])
        out = _rms_scale(out) * nw_ref[:, cols[h]]
        y_ref[0, rows, cols[h]] = (out * (gr * _sigmoid(gr))).astype(BF16)


def _hgrn_kernel(lbl_ref, nw_ref, hg_ref, y_ref, st_ref, b_scr, q_scr, k_scr, a_scr):
    t_blk = HGRN_BLOCK

    @pl.when(pl.program_id(1) == 0)
    def _():
        st_ref[...] = jnp.zeros_like(st_ref)

    lbl = lbl_ref[...]
    mx = jnp.max(lbl, axis=0, keepdims=True)
    e = jnp.exp(lbl - mx)
    lb_all = e[0:1, :] / jnp.sum(e, axis=0, keepdims=True)

    row = lax.broadcasted_iota(jnp.int32, (t_blk, t_blk), 0)
    col = lax.broadcasted_iota(jnp.int32, (t_blk, t_blk), 1)
    tri = (col <= row).astype(BF16)
    pair_key = jnp.zeros((t_blk, t_blk), jnp.int32)
    half = HGRN_SUB
    while half < t_blk:
        pair_key = jnp.where(jnp.logical_and((row ^ col) >= half, col < row), half, pair_key)
        half *= 2

    def body(i, carry):
        base = pl.multiple_of(i * t_blk, t_blk)
        _hgrn_blocks(base, lb_all, nw_ref, hg_ref, y_ref, st_ref, b_scr, q_scr, k_scr, a_scr,
                     tri, pair_key)
        return carry

    lax.fori_loop(0, hg_ref.shape[1] // t_blk, body, 0)


def _hgrn2(hg3, lb_logits, norm_w, rows_per_step=512):
    batch, seq, _ = hg3.shape
    blk_scratch = pltpu.VMEM((HGRN_HEADS, HGRN_BLOCK, HGRN_DK), F32)
    return pl.pallas_call(
        _hgrn_kernel,
        grid=(batch, seq // rows_per_step),
        in_specs=[_resident(lb_logits.shape), _resident((1, HGRN_WIDTH)),
                  pl.BlockSpec((1, rows_per_step, 4 * HGRN_WIDTH), lambda b, t: (b, t, 0))],
        out_specs=pl.BlockSpec((1, rows_per_step, HGRN_WIDTH), lambda b, t: (b, t, 0)),
        out_shape=jax.ShapeDtypeStruct((batch, seq, HGRN_WIDTH), BF16),
        scratch_shapes=[pltpu.VMEM((HGRN_HEADS, HGRN_DK, HGRN_DK), F32),
                        blk_scratch, blk_scratch, blk_scratch,
                        pltpu.VMEM((HGRN_HEADS, HGRN_BLOCK, HGRN_BLOCK), F32)],
        compiler_params=pltpu.CompilerParams(
            dimension_semantics=("arbitrary", "arbitrary"), vmem_limit_bytes=VMEM_LIMIT),
        name="hgrn2",
    )(lb_logits, norm_w, hg3)


def _out_ffn_kernel(ya_ref, yb_ref, x_ref, wo_ref, n2_ref, wgu_ref, wd_ref, nf_ref, out_ref, *, chunk):
    mixed = jnp.concatenate([ya_ref[0, j].astype(BF16) for j in range(ATTN_WIDTH // LANES)]
                            + [yb_ref[0]], axis=1)
    h = x_ref[0] + _dot(mixed, wo_ref[0].astype(BF16))
    u = (_rms_scale(h) * n2_ref[...]).astype(BF16)
    ffn = jnp.zeros_like(h)
    for c in range(FFN_HIDDEN // chunk):
        cols = slice(c * chunk, (c + 1) * chunk)
        up_cols = slice(FFN_HIDDEN + c * chunk, FFN_HIDDEN + (c + 1) * chunk)
        gate = _dot(u, wgu_ref[0, :, cols].astype(BF16))
        up = _dot(u, wgu_ref[0, :, up_cols].astype(BF16))
        act = (gate * _sigmoid(gate) * up).astype(BF16)
        ffn = ffn + _dot(act, wd_ref[0, cols, :].astype(BF16))
    out_ref[0] = _rms_scale(h + ffn) * nf_ref[...]


def _out_ffn(ya, yb, x, w_out, norm2_w, w_gate_up, w_down, final_w, tm=512, chunk=256):
    batch, seq, _ = x.shape
    n_pair = ATTN_WIDTH // LANES
    row_spec = lambda width: pl.BlockSpec((1, tm, width), lambda b, i: (b, i, 0))
    return pl.pallas_call(
        functools.partial(_out_ffn_kernel, chunk=chunk),
        grid=(batch, seq // tm),
        in_specs=[pl.BlockSpec((1, n_pair, tm, LANES), lambda b, i: (b, 0, i, 0)),
                  row_spec(HGRN_WIDTH), row_spec(D_MODEL),
                  _resident(w_out.shape), _resident((1, D_MODEL)), _resident(w_gate_up.shape),
                  _resident(w_down.shape), _resident((1, D_MODEL))],
        out_specs=row_spec(D_MODEL),
        out_shape=jax.ShapeDtypeStruct((batch, seq, D_MODEL), F32),
        compiler_params=pltpu.CompilerParams(
            dimension_semantics=("arbitrary", "arbitrary"), vmem_limit_bytes=VMEM_LIMIT),
        name="out_proj_ffn",
    )(ya, yb, x, w_out, norm2_w, w_gate_up, w_down, final_w)


def kernel(x, norm1_w, w_in, lb_logits, hgrn_norm_w, w_out, norm2_w, w_gate_up, w_down, final_norm_w):
    batch, seq, d_model = x.shape
    assert d_model == D_MODEL and norm1_w.shape[0] == 1
    assert all(w // d == ATTN_BLOCK for w, d in DILATED_PAIRS)
    assert tuple(d for _, d in DILATED_PAIRS) == (1, 4, 16) and seq % (PLANES * ATTN_BLOCK) == 0
    cos_t, sin_t = _rope_tables(seq)
    q, k, v, hg = _in_proj(x, norm1_w, w_in, cos_t, sin_t)
    ya = _dilated_attention(q, k, v)
    yb = _hgrn2(hg, lb_logits, hgrn_norm_w)
    return _out_ffn(ya, yb, x, w_out, norm2_w, w_gate_up, w_down, final_norm_w.reshape(1, d_model))
```

```python
import functools

import numpy as np
import jax
import jax.numpy as jnp
from jax import lax
from jax.experimental import pallas as pl
from jax.experimental.pallas import tpu as pltpu

D_MODEL = 1024
ATTN_WIDTH = 512
ATTN_HEADS = 8
HEAD_DIM = 64
DILATED_PAIRS = ((128, 1), (512, 4), (2048, 16))
ATTN_BLOCK = 128
ROPE_THETA = 10000.0
HGRN_WIDTH = 512
HGRN_DK = 128
HGRN_HEADS = 4
FFN_HIDDEN = 2816
IN_PROJ_WIDTH = 3 * ATTN_WIDTH + 4 * HGRN_WIDTH
NORM_EPS = 1e-6

LANES = 128
PLANES = 16
ATTN_SKEW_SCORES = 2
ATTN_SKEW_FINISH = 2
IN_PROJ_CHUNK = 256
HGRN_BLOCK = 128
HGRN_SUB = 8
VMEM_LIMIT = 56 * 1024 * 1024

F32 = jnp.float32
BF16 = jnp.bfloat16


def _dot(a, b):
    return jnp.dot(a, b, preferred_element_type=F32)


def _dot_nt(a, b):
    return lax.dot_general(a, b, (((1,), (1,)), ((), ())), preferred_element_type=F32)


def _dot_tn(a, b):
    return lax.dot_general(a, b, (((0,), (0,)), ((), ())), preferred_element_type=F32)


def _rms_scale(x):
    return x * lax.rsqrt(jnp.mean(x * x, axis=-1, keepdims=True) + NORM_EPS)


def _sigmoid(x):
    return 1.0 / (1.0 + jnp.exp(-x))


def _resident(shape):
    return pl.BlockSpec(shape, lambda *_: (0,) * len(shape), pipeline_mode=pl.Buffered(1))


def _rope_tables(seq):
    half = HEAD_DIM // 2
    inv_freq = ROPE_THETA ** (-np.arange(half, dtype=np.float64) / half)
    ang = np.arange(seq, dtype=np.float64)[:, None] * inv_freq[None, :]
    cos, sin = np.cos(ang), np.sin(ang)
    cos_t = np.concatenate([cos, cos, cos, cos], axis=1)
    sin_t = np.concatenate([-sin, sin, -sin, sin], axis=1)
    return jnp.asarray(cos_t, F32), jnp.asarray(sin_t, F32)


def _rotate_half_pairs(t, cos, sin):
    lane = lax.broadcasted_iota(jnp.int32, t.shape, 1)
    fwd = pltpu.roll(t, LANES - HEAD_DIM // 2, 1)
    bwd = pltpu.roll(t, HEAD_DIM // 2, 1)
    partner = jnp.where((lane & (HEAD_DIM // 2)) == 0, fwd, bwd)
    return t * cos + partner * sin


def _block_scores(q, keys, head_a):
    qb = q.astype(BF16)
    zero = jnp.zeros_like(qb)
    lhs = jnp.concatenate([jnp.where(head_a, qb, zero), jnp.where(head_a, zero, qb)], axis=0)
    return _dot_nt(lhs, keys)


def _block_softmax_pv(s, bias, vals_a, vals_b):
    blk = ATTN_BLOCK
    accs, maxes = [], []
    for half, vals in enumerate((vals_a, vals_b)):
        sh = s[half * blk:(half + 1) * blk] + bias
        m = jnp.max(sh, axis=-1, keepdims=True)
        accs.append(_dot(jnp.exp2(sh - m).astype(BF16), vals))
        maxes.append(m)
    return accs, maxes


def _block_finish(accs, maxes, head_a):
    out = jnp.where(head_a, accs[0], accs[1])
    den = pltpu.roll(jnp.where(head_a, accs[1], accs[0]), HEAD_DIM, 1)
    mx = jnp.where(head_a, maxes[0], maxes[1])
    return out, den, mx


def _block_row_index(dilation):
    per_plane = ATTN_BLOCK * dilation // PLANES
    shift = per_plane.bit_length() - 1
    return lambda i: ((i & (per_plane - 1)) * (PLANES // dilation)) + (i >> shift)


def _attn_kernel(q_ref, kp_ref, kc_ref, vp_ref, vc_ref, y_ref, o_scr, d_scr, m_scr, bias_scr):
    blk = ATTN_BLOCK
    dilations = (16, 4, 1)
    head_a = lax.broadcasted_iota(jnp.int32, (blk, LANES), 1) < HEAD_DIM

    @pl.when((pl.program_id(0) == 0) & (pl.program_id(1) == 0) & (pl.program_id(2) == 0))
    def _():
        row = lax.broadcasted_iota(jnp.int32, (blk, 2 * blk), 0)
        col = lax.broadcasted_iota(jnp.int32, (blk, 2 * blk), 1)
        in_prev = col < blk
        col = col & (blk - 1)
        for pat, dilation in enumerate(dilations):
            index_of_row = _block_row_index(dilation)
            delta = index_of_row(row) - index_of_row(col)
            for variant, prev_offset in enumerate((4 * blk, blk)):
                d = jnp.where(in_prev, delta + prev_offset, delta)
                bias_scr[variant, pat] = jnp.where(jnp.logical_and(d >= 0, d <= blk), 0.0, -jnp.inf)

    has_prev = (pl.program_id(2) > 0).astype(jnp.int32)

    def gather(ref, planes, lo, size):
        parts = [ref[0, 0, r, lo:lo + size, :] for r in planes]
        return parts[0] if len(parts) == 1 else jnp.concatenate(parts, axis=0)

    def keys_values(k_ref, v_ref, planes, lo, size):
        k = gather(k_ref, planes, lo, size).astype(BF16)
        v = gather(v_ref, planes, lo, size).astype(BF16)
        one = jnp.ones_like(v)
        return k, jnp.where(head_a, v, one), jnp.where(head_a, one, v)

    blocks = []
    for pat, dilation in enumerate(dilations):
        per_plane = blk * dilation // PLANES
        for res in range(dilation):
            planes = list(range(res, PLANES, dilation))
            blocks += [(pat, planes, per_plane, n) for n in range(blk // per_plane)]

    carried = {}

    def scores_stage(pat, planes, per_plane, n):
        lo = n * per_plane
        prev = keys_values(kp_ref, vp_ref, planes, blk - per_plane, per_plane) if n == 0 else carried["kv"]
        cur = carried["kv"] = keys_values(kc_ref, vc_ref, planes, lo, per_plane)
        keys, vals_a, vals_b = (jnp.concatenate([p, c], axis=0) for p, c in zip(prev, cur))
        return _block_scores(gather(q_ref, planes, lo, per_plane), keys, head_a), vals_a, vals_b

    def softmax_stage(pat, planes, per_plane, n, s, vals_a, vals_b):
        bias = bias_scr[has_prev, pat] if n == 0 else bias_scr[1, pat]
        return _block_softmax_pv(s, bias, vals_a, vals_b)

    def finish_stage(pat, planes, per_plane, n, accs, maxes):
        lo = n * per_plane
        out, den, mx = _block_finish(accs, maxes, head_a)
        for j, r in enumerate(planes):
            rows = slice(j * per_plane, (j + 1) * per_plane)
            o_scr[pat, r, lo:lo + per_plane, :] = out[rows]
            d_scr[pat, r, lo:lo + per_plane, :] = den[rows]
            m_scr[pat, r, lo:lo + per_plane, :] = mx[rows]

    in_scores, in_softmax = {}, {}
    for step in range(len(blocks) + ATTN_SKEW_SCORES + ATTN_SKEW_FINISH):
        if step < len(blocks):
            in_scores[step] = scores_stage(*blocks[step])
        mid = step - ATTN_SKEW_SCORES
        if 0 <= mid < len(blocks):
            in_softmax[mid] = softmax_stage(*blocks[mid], *in_scores.pop(mid))
        last = mid - ATTN_SKEW_FINISH
        if 0 <= last < len(blocks):
            finish_stage(*blocks[last], *in_softmax.pop(last))

    for r in range(PLANES):
        m = jnp.maximum(jnp.maximum(m_scr[0, r], m_scr[1, r]), m_scr[2, r])
        num = den = None
        for pat in range(len(dilations)):
            w = jnp.exp2(m_scr[pat, r] - m)
            num = w * o_scr[pat, r] if num is None else num + w * o_scr[pat, r]
            den = w * d_scr[pat, r] if den is None else den + w * d_scr[pat, r]
        y_ref[0, 0, pl.ds(r, blk, stride=PLANES), :] = num / den


def _dilated_attention(q, k, v):
    batch, n_pair, planes, rows, lanes = q.shape
    blk = ATTN_BLOCK
    tile = planes * blk
    cur = pl.BlockSpec((1, 1, planes, blk, lanes), lambda b, h, t: (b, h, 0, t, 0))
    prev = pl.BlockSpec((1, 1, planes, blk, lanes), lambda b, h, t: (b, h, 0, jnp.maximum(t - 1, 0), 0))
    return pl.pallas_call(
        _attn_kernel,
        grid=(batch, n_pair, rows // blk),
        in_specs=[cur, prev, cur, prev, cur],
        out_specs=pl.BlockSpec((1, 1, tile, lanes), lambda b, h, t: (b, h, t, 0)),
        out_shape=jax.ShapeDtypeStruct((batch, n_pair, planes * rows, lanes), F32),
        scratch_shapes=[pltpu.VMEM((len(DILATED_PAIRS), planes, blk, lanes), F32)] * 3
        + [pltpu.VMEM((2, len(DILATED_PAIRS), blk, 2 * blk), F32)],
        compiler_params=pltpu.CompilerParams(
            dimension_semantics=("arbitrary",) * 3, vmem_limit_bytes=VMEM_LIMIT),
        name="dilated_attn",
    )(q, k, k, v, v)


def _hgrn_blocks(base, lb_all, nw_ref, hg_ref, y_ref, st_ref, b_scr, q_scr, k_scr, a_scr, tri, pair_key):
    t_blk = HGRN_BLOCK
    dk = HGRN_DK
    sub = HGRN_SUB
    w = HGRN_WIDTH
    rows = pl.ds(base, t_blk)
    heads = range(HGRN_HEADS)
    cols = [slice(h * dk, (h + 1) * dk) for h in heads]
    qf, key, v_bf, b, o = {}, {}, {}, {}, {}

    for h in heads:
        lb = lb_all[:, cols[h]]
        qr = hg_ref[rows, h * dk:(h + 1) * dk]
        fl = hg_ref[rows, w + h * dk:w + (h + 1) * dk]
        f = lb + (1.0 - lb) * _sigmoid(fl)
        lf = jnp.log2(f)
        key[h] = 1.0 - f
        qf[h] = qr * _sigmoid(qr)
        v_bf[h] = hg_ref[rows, 2 * w + h * dk:2 * w + (h + 1) * dk].astype(BF16)
        hi = lf.astype(BF16)
        r1 = lf - hi.astype(F32)
        mid = r1.astype(BF16)
        lo = (r1 - mid.astype(F32)).astype(BF16)
        b[h] = _dot(tri, hi) + _dot(tri, mid) + _dot(tri, lo)
        b_scr[h] = b[h]
        q_scr[h] = qf[h]
        k_scr[h] = jnp.log2(key[h]) - b[h]
        yield

    for h in heads:
        b_last = b_scr[h, t_blk - 1:t_blk, :]
        st = st_ref[h]
        q_inter = (qf[h] * jnp.exp2(b[h])).astype(BF16)
        o[h] = _dot_nt(q_inter, st.astype(BF16))
        k_upd = (key[h] * jnp.exp2(b_last - b[h])).astype(BF16)
        st_ref[h] = st * jnp.exp2(b_last) + _dot_tn(v_bf[h], k_upd)
    yield

    sub_id = lax.broadcasted_iota(jnp.int32, (sub, dk), 0)
    lane_id = lax.broadcasted_iota(jnp.int32, (sub, t_blk), 1)
    for c in range(t_blk // sub):
        lo_row = c * sub
        for h in heads:
            b_t = b_scr[h, lo_row:lo_row + sub, :]
            q_t = q_scr[h, lo_row:lo_row + sub, :]
            blk_rows = jnp.zeros((sub, t_blk), F32)
            for s in range(sub):
                c_s = k_scr[h, lo_row + s:lo_row + s + 1, :]
                dec_k = jnp.exp2(jnp.where(sub_id >= s, b_t + c_s, -jnp.inf))
                col = jnp.sum(q_t * dec_k, axis=-1, keepdims=True)
                blk_rows = jnp.where(lane_id == lo_row + s, col, blk_rows)
            a_scr[h, lo_row:lo_row + sub, :] = blk_rows
        if c % 2 == 1:
            yield

    a_mat = {h: a_scr[h] for h in heads}
    rid = lax.broadcasted_iota(jnp.int32, (t_blk, dk), 0)
    half = sub
    while half < t_blk:
        span = 2 * half
        upper = (rid & half) != 0
        for h in heads:
            refs = [jnp.broadcast_to(b_scr[h, p * span + half - 1:p * span + half, :], (span, dk))
                    for p in range(t_blk // span)]
            d = b[h] - (refs[0] if len(refs) == 1 else jnp.concatenate(refs, axis=0))
            x = (jnp.where(upper, qf[h], key[h]) * jnp.exp2(jnp.where(upper, d, -d))).astype(BF16)
            a_mat[h] = jnp.where(pair_key == half, _dot_nt(x, x), a_mat[h])
        half = span
        yield

    for h in heads:
        gr = hg_ref[rows, 3 * w + h * dk:3 * w + (h + 1) * dk]
        out = o[h] + _dot(a_mat[h].astype(BF16), v_bf[h])
        out = _rms_scale(out) * nw_ref[:, cols[h]]
        y_ref[rows, cols[h]] = (out * (gr * _sigmoid(gr))).astype(BF16)


def _in_proj_hgrn_kernel(x_ref, n1_ref, w_ref, cos_ref, sin_ref, lbl_ref, nw_ref,
                         q_ref, k_ref, v_ref, y_ref,
                         hg_scr, stage_ref, st_ref, b_scr, q_scr, k_scr, a_scr, *, tiles_per_seq):
    t_blk = HGRN_BLOCK
    step = pl.program_id(0)
    write_slot = step & 1

    @pl.when(step == 0)
    def _():
        hg_scr[1] = jnp.zeros_like(hg_scr[1])

    @pl.when((step == 0) | (step % tiles_per_seq == 1))
    def _():
        st_ref[...] = jnp.zeros_like(st_ref)

    u = (_rms_scale(x_ref[0]) * n1_ref[...]).astype(BF16)
    cos = cos_ref[...]
    sin = sin_ref[...]
    a = ATTN_WIDTH
    n_pair = a // LANES
    hg_out = hg_scr.at[write_slot]
    q_scale = HEAD_DIM ** -0.5 * float(np.log2(np.e))
    rows_per_plane = stage_ref.shape[1] // PLANES
    pieces = []

    def project(col, width):
        return _dot(u, w_ref[0, :, col:col + width].astype(BF16))

    def attn_piece(i, j0, rotary, scale):
        def run():
            p = project(i * a + j0 * LANES, IN_PROJ_CHUNK)
            for jj in range(IN_PROJ_CHUNK // LANES):
                t = p[:, jj * LANES:(jj + 1) * LANES]
                if rotary:
                    t = _rotate_half_pairs(t, cos, sin)
                stage_ref[i * n_pair + j0 + jj] = t if scale is None else t * scale
        return run

    def plane_piece(i, out_ref, j):
        def run():
            for r in range(PLANES):
                out_ref[0, j, r] = stage_ref[i * n_pair + j, pl.ds(r, rows_per_plane, stride=PLANES), :]
        return run

    def hg_piece(c):
        def run():
            hg_out[:, c * IN_PROJ_CHUNK:(c + 1) * IN_PROJ_CHUNK] = project(3 * a + c * IN_PROJ_CHUNK,
                                                                           IN_PROJ_CHUNK)
        return run

    for i, (out_ref, rotary, scale) in enumerate(((q_ref, True, q_scale), (k_ref, True, None),
                                                   (v_ref, False, None))):
        for j0 in range(0, n_pair, IN_PROJ_CHUNK // LANES):
            pieces.append(attn_piece(i, j0, rotary, scale))
        pieces += [plane_piece(i, out_ref, j) for j in range(n_pair)]
    pieces += [hg_piece(c) for c in range(4 * HGRN_WIDTH // IN_PROJ_CHUNK)]

    lbl = lbl_ref[...]
    mx = jnp.max(lbl, axis=0, keepdims=True)
    e = jnp.exp(lbl - mx)
    lb_all = e[0:1, :] / jnp.sum(e, axis=0, keepdims=True)

    row = lax.broadcasted_iota(jnp.int32, (t_blk, t_blk), 0)
    col = lax.broadcasted_iota(jnp.int32, (t_blk, t_blk), 1)
    tri = (col <= row).astype(BF16)
    pair_key = jnp.zeros((t_blk, t_blk), jnp.int32)
    half = HGRN_SUB
    while half < t_blk:
        pair_key = jnp.where(jnp.logical_and((row ^ col) >= half, col < row), half, pair_key)
        half *= 2

    hg_in = hg_scr.at[1 - write_slot]
    y_out = y_ref.at[0]
    n_blocks = hg_scr.shape[1] // t_blk
    stages = [_hgrn_blocks(i * t_blk, lb_all, nw_ref, hg_in, y_out, st_ref, b_scr, q_scr, k_scr, a_scr,
                           tri, pair_key) for i in range(n_blocks)]
    for gen in stages:
        for _ in gen:
            if pieces:
                pieces.pop(0)()
    assert not pieces


def _in_proj_hgrn(x, norm1_w, w_in, cos_t, sin_t, lb_logits, hgrn_norm_w, tm=512):
    batch, seq, _ = x.shape
    n_pair = ATTN_WIDTH // LANES
    tiles_per_seq = seq // tm
    n_tiles = batch * tiles_per_seq

    def proj_tile(s):
        s = jnp.minimum(s, n_tiles - 1)
        return s // tiles_per_seq, s % tiles_per_seq

    def hgrn_tile(s):
        s = jnp.maximum(s - 1, 0)
        return s // tiles_per_seq, s % tiles_per_seq

    plane_spec = pl.BlockSpec((1, n_pair, PLANES, tm // PLANES, LANES),
                              lambda s: (proj_tile(s)[0], 0, 0, proj_tile(s)[1], 0))
    plane_shape = jax.ShapeDtypeStruct((batch, n_pair, PLANES, seq // PLANES, LANES), F32)
    tab_spec = pl.BlockSpec((tm, LANES), lambda s: (proj_tile(s)[1], 0))
    blk_scratch = pltpu.VMEM((HGRN_HEADS, HGRN_BLOCK, HGRN_DK), F32)
    return pl.pallas_call(
        functools.partial(_in_proj_hgrn_kernel, tiles_per_seq=tiles_per_seq),
        grid=(n_tiles + 1,),
        in_specs=[pl.BlockSpec((1, tm, D_MODEL), lambda s: (*proj_tile(s), 0)), _resident((1, D_MODEL)),
                  _resident(w_in.shape), tab_spec, tab_spec,
                  _resident(lb_logits.shape), _resident((1, HGRN_WIDTH))],
        out_specs=[plane_spec] * 3 + [pl.BlockSpec((1, tm, HGRN_WIDTH), lambda s: (*hgrn_tile(s), 0))],
        out_shape=[plane_shape] * 3 + [jax.ShapeDtypeStruct((batch, seq, HGRN_WIDTH), BF16)],
        scratch_shapes=[pltpu.VMEM((2, tm, 4 * HGRN_WIDTH), F32),
                        pltpu.VMEM((3 * n_pair, tm, LANES), F32),
                        pltpu.VMEM((HGRN_HEADS, HGRN_DK, HGRN_DK), F32),
                        blk_scratch, blk_scratch, blk_scratch,
                        pltpu.VMEM((HGRN_HEADS, HGRN_BLOCK, HGRN_BLOCK), F32)],
        compiler_params=pltpu.CompilerParams(
            dimension_semantics=("arbitrary",), vmem_limit_bytes=VMEM_LIMIT),
        name="in_proj_hgrn2",
    )(x, norm1_w, w_in, cos_t, sin_t, lb_logits, hgrn_norm_w)


def _out_ffn_kernel(ya_ref, yb_ref, x_ref, wo_ref, n2_ref, wgu_ref, wd_ref, nf_ref, out_ref, *, chunk):
    mixed = jnp.concatenate([ya_ref[0, j].astype(BF16) for j in range(ATTN_WIDTH // LANES)]
                            + [yb_ref[0]], axis=1)
    h = x_ref[0] + _dot(mixed, wo_ref[0].astype(BF16))
    u = (_rms_scale(h) * n2_ref[...]).astype(BF16)
    ffn = jnp.zeros_like(h)
    for c in range(FFN_HIDDEN // chunk):
        cols = slice(c * chunk, (c + 1) * chunk)
        up_cols = slice(FFN_HIDDEN + c * chunk, FFN_HIDDEN + (c + 1) * chunk)
        gate = _dot(u, wgu_ref[0, :, cols].astype(BF16))
        up = _dot(u, wgu_ref[0, :, up_cols].astype(BF16))
        act = (gate * _sigmoid(gate) * up).astype(BF16)
        ffn = ffn + _dot(act, wd_ref[0, cols, :].astype(BF16))
    out_ref[0] = _rms_scale(h + ffn) * nf_ref[...]


def _out_ffn(ya, yb, x, w_out, norm2_w, w_gate_up, w_down, final_w, tm=512, chunk=256):
    batch, seq, _ = x.shape
    n_pair = ATTN_WIDTH // LANES
    row_spec = lambda width: pl.BlockSpec((1, tm, width), lambda b, i: (b, i, 0))
    return pl.pallas_call(
        functools.partial(_out_ffn_kernel, chunk=chunk),
        grid=(batch, seq // tm),
        in_specs=[pl.BlockSpec((1, n_pair, tm, LANES), lambda b, i: (b, 0, i, 0)),
                  row_spec(HGRN_WIDTH), row_spec(D_MODEL),
                  _resident(w_out.shape), _resident((1, D_MODEL)), _resident(w_gate_up.shape),
                  _resident(w_down.shape), _resident((1, D_MODEL))],
        out_specs=row_spec(D_MODEL),
        out_shape=jax.ShapeDtypeStruct((batch, seq, D_MODEL), F32),
        compiler_params=pltpu.CompilerParams(
            dimension_semantics=("arbitrary", "arbitrary"), vmem_limit_bytes=VMEM_LIMIT),
        name="out_proj_ffn",
    )(ya, yb, x, w_out, norm2_w, w_gate_up, w_down, final_w)


def kernel(x, norm1_w, w_in, lb_logits, hgrn_norm_w, w_out, norm2_w, w_gate_up, w_down, final_norm_w):
    batch, seq, d_model = x.shape
    assert d_model == D_MODEL and norm1_w.shape[0] == 1
    assert all(w // d == ATTN_BLOCK for w, d in DILATED_PAIRS)
    assert tuple(d for _, d in DILATED_PAIRS) == (1, 4, 16) and seq % (PLANES * ATTN_BLOCK) == 0
    cos_t, sin_t = _rope_tables(seq)
    q, k, v, yb = _in_proj_hgrn(x, norm1_w, w_in, cos_t, sin_t, lb_logits, hgrn_norm_w)
    ya = _dilated_attention(q, k, v)
    return _out_ffn(ya, yb, x, w_out, norm2_w, w_gate_up, w_down, final_norm_w.reshape(1, d_model))
```

```python
import functools

import numpy as np
import jax
import jax.numpy as jnp
from jax import lax
from jax.experimental import pallas as pl
from jax.experimental.pallas import tpu as pltpu

D_MODEL = 1024
ATTN_WIDTH = 512
ATTN_HEADS = 8
HEAD_DIM = 64
DILATED_PAIRS = ((128, 1), (512, 4), (2048, 16))
ATTN_BLOCK = 128
ROPE_THETA = 10000.0
HGRN_WIDTH = 512
HGRN_DK = 128
HGRN_HEADS = 4
FFN_HIDDEN = 2816
IN_PROJ_WIDTH = 3 * ATTN_WIDTH + 4 * HGRN_WIDTH
NORM_EPS = 1e-6

LANES = 128
PLANES = 16
ATTN_SKEW_SCORES = 2
ATTN_SKEW_FINISH = 2
IN_PROJ_CHUNK = 512
HGRN_BLOCK = 128
HGRN_SUB = 8
VMEM_LIMIT = 56 * 1024 * 1024

F32 = jnp.float32
BF16 = jnp.bfloat16


def _dot(a, b):
    return jnp.dot(a, b, preferred_element_type=F32)


def _dot_nt(a, b):
    return lax.dot_general(a, b, (((1,), (1,)), ((), ())), preferred_element_type=F32)


def _dot_tn(a, b):
    return lax.dot_general(a, b, (((0,), (0,)), ((), ())), preferred_element_type=F32)


def _rms_scale(x):
    return x * lax.rsqrt(jnp.mean(x * x, axis=-1, keepdims=True) + NORM_EPS)


def _sigmoid(x):
    return 1.0 / (1.0 + jnp.exp(-x))


def _resident(shape):
    return pl.BlockSpec(shape, lambda *_: (0,) * len(shape), pipeline_mode=pl.Buffered(1))


def _rope_tables(seq):
    half = HEAD_DIM // 2
    inv_freq = ROPE_THETA ** (-np.arange(half, dtype=np.float64) / half)
    ang = np.arange(seq, dtype=np.float64)[:, None] * inv_freq[None, :]
    cos, sin = np.cos(ang), np.sin(ang)
    cos_t = np.concatenate([cos, cos, cos, cos], axis=1)
    sin_t = np.concatenate([-sin, sin, -sin, sin], axis=1)
    return jnp.asarray(cos_t, F32), jnp.asarray(sin_t, F32)


def _rotate_half_pairs(t, cos, sin):
    lane = lax.broadcasted_iota(jnp.int32, t.shape, 1)
    fwd = pltpu.roll(t, LANES - HEAD_DIM // 2, 1)
    bwd = pltpu.roll(t, HEAD_DIM // 2, 1)
    partner = jnp.where((lane & (HEAD_DIM // 2)) == 0, fwd, bwd)
    return t * cos + partner * sin


def _block_scores(q, keys, head_a):
    qb = q.astype(BF16)
    zero = jnp.zeros_like(qb)
    lhs = jnp.concatenate([jnp.where(head_a, qb, zero), jnp.where(head_a, zero, qb)], axis=0)
    return _dot_nt(lhs, keys)


def _block_softmax_pv(s, bias, vals_a, vals_b):
    blk = ATTN_BLOCK
    accs, maxes = [], []
    for half, vals in enumerate((vals_a, vals_b)):
        sh = s[half * blk:(half + 1) * blk] + bias
        m = jnp.max(sh, axis=-1, keepdims=True)
        accs.append(_dot(jnp.exp2(sh - m).astype(BF16), vals))
        maxes.append(m)
    return accs, maxes


def _block_finish(accs, maxes, head_a):
    out = jnp.where(head_a, accs[0], accs[1])
    den = pltpu.roll(jnp.where(head_a, accs[1], accs[0]), HEAD_DIM, 1)
    mx = jnp.where(head_a, maxes[0], maxes[1])
    return out, den, mx


def _block_row_index(dilation):
    per_plane = ATTN_BLOCK * dilation // PLANES
    shift = per_plane.bit_length() - 1
    return lambda i: ((i & (per_plane - 1)) * (PLANES // dilation)) + (i >> shift)


def _attn_kernel(q_ref, kp_ref, kc_ref, vp_ref, vc_ref, y_ref, o_scr, d_scr, m_scr, bias_scr):
    blk = ATTN_BLOCK
    dilations = (16, 4, 1)
    head_a = lax.broadcasted_iota(jnp.int32, (blk, LANES), 1) < HEAD_DIM

    @pl.when((pl.program_id(0) == 0) & (pl.program_id(1) == 0) & (pl.program_id(2) == 0))
    def _():
        row = lax.broadcasted_iota(jnp.int32, (blk, 2 * blk), 0)
        col = lax.broadcasted_iota(jnp.int32, (blk, 2 * blk), 1)
        in_prev = col < blk
        col = col & (blk - 1)
        for pat, dilation in enumerate(dilations):
            index_of_row = _block_row_index(dilation)
            delta = index_of_row(row) - index_of_row(col)
            for variant, prev_offset in enumerate((4 * blk, blk)):
                d = jnp.where(in_prev, delta + prev_offset, delta)
                bias_scr[variant, pat] = jnp.where(jnp.logical_and(d >= 0, d <= blk), 0.0, -jnp.inf)

    has_prev = (pl.program_id(2) > 0).astype(jnp.int32)

    def gather(ref, planes, lo, size):
        parts = [ref[0, 0, r, lo:lo + size, :] for r in planes]
        return parts[0] if len(parts) == 1 else jnp.concatenate(parts, axis=0)

    def keys_values(k_ref, v_ref, planes, lo, size):
        k = gather(k_ref, planes, lo, size).astype(BF16)
        v = gather(v_ref, planes, lo, size).astype(BF16)
        one = jnp.ones_like(v)
        return k, jnp.where(head_a, v, one), jnp.where(head_a, one, v)

    blocks = []
    for pat, dilation in enumerate(dilations):
        per_plane = blk * dilation // PLANES
        for res in range(dilation):
            planes = list(range(res, PLANES, dilation))
            blocks += [(pat, planes, per_plane, n) for n in range(blk // per_plane)]

    carried = {}

    def scores_stage(pat, planes, per_plane, n):
        lo = n * per_plane
        prev = keys_values(kp_ref, vp_ref, planes, blk - per_plane, per_plane) if n == 0 else carried["kv"]
        cur = carried["kv"] = keys_values(kc_ref, vc_ref, planes, lo, per_plane)
        keys, vals_a, vals_b = (jnp.concatenate([p, c], axis=0) for p, c in zip(prev, cur))
        return _block_scores(gather(q_ref, planes, lo, per_plane), keys, head_a), vals_a, vals_b

    def softmax_stage(pat, planes, per_plane, n, s, vals_a, vals_b):
        bias = bias_scr[has_prev, pat] if n == 0 else bias_scr[1, pat]
        return _block_softmax_pv(s, bias, vals_a, vals_b)

    def finish_stage(pat, planes, per_plane, n, accs, maxes):
        lo = n * per_plane
        out, den, mx = _block_finish(accs, maxes, head_a)
        for j, r in enumerate(planes):
            rows = slice(j * per_plane, (j + 1) * per_plane)
            o_scr[pat, r, lo:lo + per_plane, :] = out[rows]
            d_scr[pat, r, lo:lo + per_plane, :] = den[rows]
            m_scr[pat, r, lo:lo + per_plane, :] = mx[rows]

    in_scores, in_softmax = {}, {}
    for step in range(len(blocks) + ATTN_SKEW_SCORES + ATTN_SKEW_FINISH):
        if step < len(blocks):
            in_scores[step] = scores_stage(*blocks[step])
        mid = step - ATTN_SKEW_SCORES
        if 0 <= mid < len(blocks):
            in_softmax[mid] = softmax_stage(*blocks[mid], *in_scores.pop(mid))
        last = mid - ATTN_SKEW_FINISH
        if 0 <= last < len(blocks):
            finish_stage(*blocks[last], *in_softmax.pop(last))

    for r in range(PLANES):
        m = jnp.maximum(jnp.maximum(m_scr[0, r], m_scr[1, r]), m_scr[2, r])
        num = den = None
        for pat in range(len(dilations)):
            w = jnp.exp2(m_scr[pat, r] - m)
            num = w * o_scr[pat, r] if num is None else num + w * o_scr[pat, r]
            den = w * d_scr[pat, r] if den is None else den + w * d_scr[pat, r]
        y_ref[0, 0, pl.ds(r, blk, stride=PLANES), :] = num / den


def _dilated_attention(q, k, v):
    batch, n_pair, planes, rows, lanes = q.shape
    blk = ATTN_BLOCK
    tile = planes * blk
    cur = pl.BlockSpec((1, 1, planes, blk, lanes), lambda b, h, t: (b, h, 0, t, 0))
    prev = pl.BlockSpec((1, 1, planes, blk, lanes), lambda b, h, t: (b, h, 0, jnp.maximum(t - 1, 0), 0))
    return pl.pallas_call(
        _attn_kernel,
        grid=(batch, n_pair, rows // blk),
        in_specs=[cur, prev, cur, prev, cur],
        out_specs=pl.BlockSpec((1, 1, tile, lanes), lambda b, h, t: (b, h, t, 0)),
        out_shape=jax.ShapeDtypeStruct((batch, n_pair, planes * rows, lanes), F32),
        scratch_shapes=[pltpu.VMEM((len(DILATED_PAIRS), planes, blk, lanes), F32)] * 3
        + [pltpu.VMEM((2, len(DILATED_PAIRS), blk, 2 * blk), F32)],
        compiler_params=pltpu.CompilerParams(
            dimension_semantics=("arbitrary",) * 3, vmem_limit_bytes=VMEM_LIMIT),
        name="dilated_attn",
    )(q, k, k, v, v)


def _hgrn_blocks(base, lb_all, nw_ref, hg_ref, y_ref, st_ref, b_scr, q_scr, k_scr, a_scr, tri, pair_key):
    t_blk = HGRN_BLOCK
    dk = HGRN_DK
    sub = HGRN_SUB
    w = HGRN_WIDTH
    rows = pl.ds(base, t_blk)
    heads = range(HGRN_HEADS)
    cols = [slice(h * dk, (h + 1) * dk) for h in heads]
    qf, key, v_bf, b, o = {}, {}, {}, {}, {}

    for h in heads:
        lb = lb_all[:, cols[h]]
        qr = hg_ref[rows, h * dk:(h + 1) * dk]
        fl = hg_ref[rows, w + h * dk:w + (h + 1) * dk]
        f = lb + (1.0 - lb) * _sigmoid(fl)
        lf = jnp.log2(f)
        key[h] = 1.0 - f
        qf[h] = qr * _sigmoid(qr)
        v_bf[h] = hg_ref[rows, 2 * w + h * dk:2 * w + (h + 1) * dk].astype(BF16)
        hi = lf.astype(BF16)
        r1 = lf - hi.astype(F32)
        mid = r1.astype(BF16)
        lo = (r1 - mid.astype(F32)).astype(BF16)
        b[h] = _dot(tri, hi) + _dot(tri, mid) + _dot(tri, lo)
        b_scr[h] = b[h]
        q_scr[h] = qf[h]
        k_scr[h] = jnp.log2(key[h]) - b[h]
        yield

    for h in heads:
        b_last = b_scr[h, t_blk - 1:t_blk, :]
        st = st_ref[h]
        q_inter = (qf[h] * jnp.exp2(b[h])).astype(BF16)
        o[h] = _dot_nt(q_inter, st.astype(BF16))
        k_upd = (key[h] * jnp.exp2(b_last - b[h])).astype(BF16)
        st_ref[h] = st * jnp.exp2(b_last) + _dot_tn(v_bf[h], k_upd)
    yield

    sub_id = lax.broadcasted_iota(jnp.int32, (sub, t_blk), 0)
    lane_id = lax.broadcasted_iota(jnp.int32, (sub, t_blk), 1)
    keep = [jnp.logical_and((lane_id & (sub - 1)) == s, sub_id >= s) for s in range(sub)]
    for c in range(t_blk // sub):
        lo_row = c * sub
        in_chunk = (lane_id & -sub) == lo_row
        for h in heads:
            b_t = b_scr[h, lo_row:lo_row + sub, :]
            q_t = q_scr[h, lo_row:lo_row + sub, :]
            blk_rows = jnp.zeros((sub, t_blk), F32)
            for s in range(sub):
                c_s = k_scr[h, lo_row + s:lo_row + s + 1, :]
                col = jnp.sum(q_t * jnp.exp2(b_t + c_s), axis=-1, keepdims=True)
                blk_rows = jnp.where(keep[s], col, blk_rows)
            a_scr[h, lo_row:lo_row + sub, :] = jnp.where(in_chunk, blk_rows, 0.0)
        if c % 2 == 1:
            yield

    a_mat = {h: a_scr[h] for h in heads}
    rid = lax.broadcasted_iota(jnp.int32, (t_blk, dk), 0)
    half = sub
    while half < t_blk:
        span = 2 * half
        upper = (rid & half) != 0
        for h in heads:
            refs = [jnp.broadcast_to(b_scr[h, p * span + half - 1:p * span + half, :], (span, dk))
                    for p in range(t_blk // span)]
            d = b[h] - (refs[0] if len(refs) == 1 else jnp.concatenate(refs, axis=0))
            x = (jnp.where(upper, qf[h], key[h]) * jnp.exp2(jnp.where(upper, d, -d))).astype(BF16)
            a_mat[h] = jnp.where(pair_key == half, _dot_nt(x, x), a_mat[h])
        half = span
        yield

    for h in heads:
        gr = hg_ref[rows, 3 * w + h * dk:3 * w + (h + 1) * dk]
        out = o[h] + _dot(a_mat[h].astype(BF16), v_bf[h])
        out = _rms_scale(out) * nw_ref[:, cols[h]]
        y_ref[rows, cols[h]] = (out * (gr * _sigmoid(gr))).astype(BF16)


def _in_proj_hgrn_kernel(x_ref, n1_ref, w_ref, cos_ref, sin_ref, lbl_ref, nw_ref,
                         q_ref, k_ref, v_ref, y_ref,
                         hg_scr, w_scr, stage_ref, st_ref, b_scr, q_scr, k_scr, a_scr, *, tiles_per_seq):
    t_blk = HGRN_BLOCK
    step = pl.program_id(0)
    write_slot = step & 1

    @pl.when(step == 0)
    def _():
        hg_scr[1] = jnp.zeros_like(hg_scr[1])
        for col in range(0, IN_PROJ_WIDTH, IN_PROJ_CHUNK):
            w_scr[:, col:col + IN_PROJ_CHUNK] = w_ref[0, :, col:col + IN_PROJ_CHUNK].astype(BF16)

    @pl.when((step == 0) | (step % tiles_per_seq == 1))
    def _():
        st_ref[...] = jnp.zeros_like(st_ref)

    u = (_rms_scale(x_ref[0]) * n1_ref[...]).astype(BF16)
    cos = cos_ref[...]
    sin = sin_ref[...]
    a = ATTN_WIDTH
    n_pair = a // LANES
    hg_out = hg_scr.at[write_slot]
    q_scale = HEAD_DIM ** -0.5 * float(np.log2(np.e))
    rows_per_plane = stage_ref.shape[1] // PLANES
    pieces = []

    def project(col, width):
        return _dot(u, w_scr[:, col:col + width])

    def attn_piece(i, j0, rotary, scale):
        def run():
            p = project(i * a + j0 * LANES, IN_PROJ_CHUNK)
            for jj in range(IN_PROJ_CHUNK // LANES):
                t = p[:, jj * LANES:(jj + 1) * LANES]
                if rotary:
                    t = _rotate_half_pairs(t, cos, sin)
                stage_ref[i * n_pair + j0 + jj] = t if scale is None else t * scale
        return run

    def plane_piece(i, out_ref, j):
        def run():
            for r in range(PLANES):
                out_ref[0, j, r] = stage_ref[i * n_pair + j, pl.ds(r, rows_per_plane, stride=PLANES), :]
        return run

    def hg_piece(c):
        def run():
            hg_out[:, c * IN_PROJ_CHUNK:(c + 1) * IN_PROJ_CHUNK] = project(3 * a + c * IN_PROJ_CHUNK,
                                                                           IN_PROJ_CHUNK)
        return run

    for i, (out_ref, rotary, scale) in enumerate(((q_ref, True, q_scale), (k_ref, True, None),
                                                   (v_ref, False, None))):
        for j0 in range(0, n_pair, IN_PROJ_CHUNK // LANES):
            pieces.append(attn_piece(i, j0, rotary, scale))
        pieces += [plane_piece(i, out_ref, j) for j in range(n_pair)]
    pieces += [hg_piece(c) for c in range(4 * HGRN_WIDTH // IN_PROJ_CHUNK)]

    lbl = lbl_ref[...]
    mx = jnp.max(lbl, axis=0, keepdims=True)
    e = jnp.exp(lbl - mx)
    lb_all = e[0:1, :] / jnp.sum(e, axis=0, keepdims=True)

    row = lax.broadcasted_iota(jnp.int32, (t_blk, t_blk), 0)
    col = lax.broadcasted_iota(jnp.int32, (t_blk, t_blk), 1)
    tri = (col <= row).astype(BF16)
    pair_key = jnp.zeros((t_blk, t_blk), jnp.int32)
    half = HGRN_SUB
    while half < t_blk:
        pair_key = jnp.where(jnp.logical_and((row ^ col) >= half, col < row), half, pair_key)
        half *= 2

    hg_in = hg_scr.at[1 - write_slot]
    y_out = y_ref.at[0]
    n_blocks = hg_scr.shape[1] // t_blk
    stages = [_hgrn_blocks(i * t_blk, lb_all, nw_ref, hg_in, y_out, st_ref, b_scr, q_scr, k_scr, a_scr,
                           tri, pair_key) for i in range(n_blocks)]
    for gen in stages:
        for _ in gen:
            if pieces:
                pieces.pop(0)()
    assert not pieces


def _in_proj_hgrn(x, norm1_w, w_in, cos_t, sin_t, lb_logits, hgrn_norm_w, tm=512):
    batch, seq, _ = x.shape
    n_pair = ATTN_WIDTH // LANES
    tiles_per_seq = seq // tm
    n_tiles = batch * tiles_per_seq

    def proj_tile(s):
        s = jnp.minimum(s, n_tiles - 1)
        return s // tiles_per_seq, s % tiles_per_seq

    def hgrn_tile(s):
        s = jnp.maximum(s - 1, 0)
        return s // tiles_per_seq, s % tiles_per_seq

    plane_spec = pl.BlockSpec((1, n_pair, PLANES, tm // PLANES, LANES),
                              lambda s: (proj_tile(s)[0], 0, 0, proj_tile(s)[1], 0))
    plane_shape = jax.ShapeDtypeStruct((batch, n_pair, PLANES, seq // PLANES, LANES), F32)
    tab_spec = pl.BlockSpec((tm, LANES), lambda s: (proj_tile(s)[1], 0))
    blk_scratch = pltpu.VMEM((HGRN_HEADS, HGRN_BLOCK, HGRN_DK), F32)
    return pl.pallas_call(
        functools.partial(_in_proj_hgrn_kernel, tiles_per_seq=tiles_per_seq),
        grid=(n_tiles + 1,),
        in_specs=[pl.BlockSpec((1, tm, D_MODEL), lambda s: (*proj_tile(s), 0)), _resident((1, D_MODEL)),
                  _resident(w_in.shape), tab_spec, tab_spec,
                  _resident(lb_logits.shape), _resident((1, HGRN_WIDTH))],
        out_specs=[plane_spec] * 3 + [pl.BlockSpec((1, tm, HGRN_WIDTH), lambda s: (*hgrn_tile(s), 0))],
        out_shape=[plane_shape] * 3 + [jax.ShapeDtypeStruct((batch, seq, HGRN_WIDTH), BF16)],
        scratch_shapes=[pltpu.VMEM((2, tm, 4 * HGRN_WIDTH), F32),
                        pltpu.VMEM((D_MODEL, IN_PROJ_WIDTH), BF16),
                        pltpu.VMEM((3 * n_pair, tm, LANES), F32),
                        pltpu.VMEM((HGRN_HEADS, HGRN_DK, HGRN_DK), F32),
                        blk_scratch, blk_scratch, blk_scratch,
                        pltpu.VMEM((HGRN_HEADS, HGRN_BLOCK, HGRN_BLOCK), F32)],
        compiler_params=pltpu.CompilerParams(
            dimension_semantics=("arbitrary",), vmem_limit_bytes=VMEM_LIMIT),
        name="in_proj_hgrn2",
    )(x, norm1_w, w_in, cos_t, sin_t, lb_logits, hgrn_norm_w)


def _out_ffn_kernel(ya_ref, yb_ref, x_ref, wo_ref, n2_ref, wgu_ref, wd_ref, nf_ref, out_ref, *, chunk):
    mixed = jnp.concatenate([ya_ref[0, j].astype(BF16) for j in range(ATTN_WIDTH // LANES)]
                            + [yb_ref[0]], axis=1)
    h = x_ref[0] + _dot(mixed, wo_ref[0].astype(BF16))
    u = (_rms_scale(h) * n2_ref[...]).astype(BF16)
    ffn = jnp.zeros_like(h)
    for c in range(FFN_HIDDEN // chunk):
        cols = slice(c * chunk, (c + 1) * chunk)
        up_cols = slice(FFN_HIDDEN + c * chunk, FFN_HIDDEN + (c + 1) * chunk)
        gate = _dot(u, wgu_ref[0, :, cols].astype(BF16))
        up = _dot(u, wgu_ref[0, :, up_cols].astype(BF16))
        act = (gate * _sigmoid(gate) * up).astype(BF16)
        ffn = ffn + _dot(act, wd_ref[0, cols, :].astype(BF16))
    out_ref[0] = _rms_scale(h + ffn) * nf_ref[...]


def _out_ffn(ya, yb, x, w_out, norm2_w, w_gate_up, w_down, final_w, tm=512, chunk=256):
    batch, seq, _ = x.shape
    n_pair = ATTN_WIDTH // LANES
    row_spec = lambda width: pl.BlockSpec((1, tm, width), lambda b, i: (b, i, 0))
    return pl.pallas_call(
        functools.partial(_out_ffn_kernel, chunk=chunk),
        grid=(batch, seq // tm),
        in_specs=[pl.BlockSpec((1, n_pair, tm, LANES), lambda b, i: (b, 0, i, 0)),
                  row_spec(HGRN_WIDTH), row_spec(D_MODEL),
                  _resident(w_out.shape), _resident((1, D_MODEL)), _resident(w_gate_up.shape),
                  _resident(w_down.shape), _resident((1, D_MODEL))],
        out_specs=row_spec(D_MODEL),
        out_shape=jax.ShapeDtypeStruct((batch, seq, D_MODEL), F32),
        compiler_params=pltpu.CompilerParams(
            dimension_semantics=("arbitrary", "arbitrary"), vmem_limit_bytes=VMEM_LIMIT),
        name="out_proj_ffn",
    )(ya, yb, x, w_out, norm2_w, w_gate_up, w_down, final_w)


def kernel(x, norm1_w, w_in, lb_logits, hgrn_norm_w, w_out, norm2_w, w_gate_up, w_down, final_norm_w):
    batch, seq, d_model = x.shape
    assert d_model == D_MODEL and norm1_w.shape[0] == 1
    assert all(w // d == ATTN_BLOCK for w, d in DILATED_PAIRS)
    assert tuple(d for _, d in DILATED_PAIRS) == (1, 4, 16) and seq % (PLANES * ATTN_BLOCK) == 0
    cos_t, sin_t = _rope_tables(seq)
    q, k, v, yb = _in_proj_hgrn(x, norm1_w, w_in, cos_t, sin_t, lb_logits, hgrn_norm_w)
    ya = _dilated_attention(q, k, v)
    return _out_ffn(ya, yb, x, w_out, norm2_w, w_gate_up, w_down, final_norm_w.reshape(1, d_model))
```

```python
import functools

import numpy as np
import jax
import jax.numpy as jnp
from jax import lax
from jax.experimental import pallas as pl
from jax.experimental.pallas import tpu as pltpu

D_MODEL = 1024
ATTN_WIDTH = 512
ATTN_HEADS = 8
HEAD_DIM = 64
DILATED_PAIRS = ((128, 1), (512, 4), (2048, 16))
ATTN_BLOCK = 128
ROPE_THETA = 10000.0
HGRN_WIDTH = 512
HGRN_DK = 128
HGRN_HEADS = 4
FFN_HIDDEN = 2816
IN_PROJ_WIDTH = 3 * ATTN_WIDTH + 4 * HGRN_WIDTH
NORM_EPS = 1e-6

LANES = 128
PLANES = 16
ATTN_SKEW_SCORES = 2
ATTN_SKEW_FINISH = 2
IN_PROJ_CHUNK = 512
HGRN_BLOCK = 128
HGRN_SUB = 8
VMEM_LIMIT = 56 * 1024 * 1024

F32 = jnp.float32
BF16 = jnp.bfloat16


def _dot(a, b):
    return jnp.dot(a, b, preferred_element_type=F32)


def _dot_nt(a, b):
    return lax.dot_general(a, b, (((1,), (1,)), ((), ())), preferred_element_type=F32)


def _dot_tn(a, b):
    return lax.dot_general(a, b, (((0,), (0,)), ((), ())), preferred_element_type=F32)


def _rms_scale(x):
    return x * lax.rsqrt(jnp.mean(x * x, axis=-1, keepdims=True) + NORM_EPS)


def _sigmoid(x):
    return 1.0 / (1.0 + jnp.exp(-x))


def _resident(shape):
    return pl.BlockSpec(shape, lambda *_: (0,) * len(shape), pipeline_mode=pl.Buffered(1))


def _rope_tables(seq):
    half = HEAD_DIM // 2
    inv_freq = ROPE_THETA ** (-np.arange(half, dtype=np.float64) / half)
    ang = np.arange(seq, dtype=np.float64)[:, None] * inv_freq[None, :]
    cos, sin = np.cos(ang), np.sin(ang)
    cos_t = np.concatenate([cos, cos, cos, cos], axis=1)
    sin_t = np.concatenate([-sin, -sin, sin, sin], axis=1)
    return jnp.asarray(cos_t, F32), jnp.asarray(sin_t, F32)


def _qk_lane_order(w):
    half = HEAD_DIM // 2
    lane = lax.broadcasted_iota(jnp.int32, w.shape, 1)
    from_right = pltpu.roll(w, LANES - half, 1)
    from_left = pltpu.roll(w, half, 1)
    return jnp.where((lane >= half) & (lane < 2 * half), from_right,
                     jnp.where((lane >= 2 * half) & (lane < 3 * half), from_left, w))


def _rotate_half_pairs(t, cos, sin):
    return t * cos + pltpu.roll(t, HEAD_DIM, 1) * sin


def _block_scores(q, keys):
    qb = q.astype(BF16)
    zero = jnp.zeros_like(qb)
    qk_head_a = (lax.broadcasted_iota(jnp.int32, qb.shape, 1) & (HEAD_DIM // 2)) == 0
    lhs = jnp.concatenate([jnp.where(qk_head_a, qb, zero), jnp.where(qk_head_a, zero, qb)], axis=0)
    return _dot_nt(lhs, keys)


def _block_softmax_pv(s, bias, vals_a, vals_b):
    blk = ATTN_BLOCK
    accs, maxes = [], []
    for half, vals in enumerate((vals_a, vals_b)):
        sh = s[half * blk:(half + 1) * blk] + bias
        m = jnp.max(sh, axis=-1, keepdims=True)
        accs.append(_dot(jnp.exp2(sh - m).astype(BF16), vals))
        maxes.append(m)
    return accs, maxes


def _block_finish(accs, maxes, head_a):
    out = jnp.where(head_a, accs[0], accs[1])
    den = pltpu.roll(jnp.where(head_a, accs[1], accs[0]), HEAD_DIM, 1)
    mx = jnp.where(head_a, maxes[0], maxes[1])
    return out, den, mx


def _block_row_index(dilation):
    per_plane = ATTN_BLOCK * dilation // PLANES
    shift = per_plane.bit_length() - 1
    return lambda i: ((i & (per_plane - 1)) * (PLANES // dilation)) + (i >> shift)


def _attn_kernel(q_ref, kp_ref, kc_ref, vp_ref, vc_ref, y_ref, o_scr, d_scr, m_scr, bias_scr):
    blk = ATTN_BLOCK
    dilations = (16, 4, 1)
    head_a = lax.broadcasted_iota(jnp.int32, (blk, LANES), 1) < HEAD_DIM

    @pl.when((pl.program_id(0) == 0) & (pl.program_id(1) == 0) & (pl.program_id(2) == 0))
    def _():
        row = lax.broadcasted_iota(jnp.int32, (blk, 2 * blk), 0)
        col = lax.broadcasted_iota(jnp.int32, (blk, 2 * blk), 1)
        in_prev = col < blk
        col = col & (blk - 1)
        for pat, dilation in enumerate(dilations):
            index_of_row = _block_row_index(dilation)
            delta = index_of_row(row) - index_of_row(col)
            for variant, prev_offset in enumerate((4 * blk, blk)):
                d = jnp.where(in_prev, delta + prev_offset, delta)
                bias_scr[variant, pat] = jnp.where(jnp.logical_and(d >= 0, d <= blk), 0.0, -jnp.inf)

    has_prev = (pl.program_id(2) > 0).astype(jnp.int32)

    def gather(ref, planes, lo, size):
        parts = [ref[0, 0, r, lo:lo + size, :] for r in planes]
        return parts[0] if len(parts) == 1 else jnp.concatenate(parts, axis=0)

    def keys_values(k_ref, v_ref, planes, lo, size):
        k = gather(k_ref, planes, lo, size).astype(BF16)
        v = gather(v_ref, planes, lo, size).astype(BF16)
        one = jnp.ones_like(v)
        return k, jnp.where(head_a, v, one), jnp.where(head_a, one, v)

    blocks = []
    for pat, dilation in enumerate(dilations):
        per_plane = blk * dilation // PLANES
        for res in range(dilation):
            planes = list(range(res, PLANES, dilation))
            blocks += [(pat, planes, per_plane, n) for n in range(blk // per_plane)]

    carried = {}

    def scores_stage(pat, planes, per_plane, n):
        lo = n * per_plane
        prev = keys_values(kp_ref, vp_ref, planes, blk - per_plane, per_plane) if n == 0 else carried["kv"]
        cur = carried["kv"] = keys_values(kc_ref, vc_ref, planes, lo, per_plane)
        keys, vals_a, vals_b = (jnp.concatenate([p, c], axis=0) for p, c in zip(prev, cur))
        return _block_scores(gather(q_ref, planes, lo, per_plane), keys), vals_a, vals_b

    def softmax_stage(pat, planes, per_plane, n, s, vals_a, vals_b):
        bias = bias_scr[has_prev, pat] if n == 0 else bias_scr[1, pat]
        return _block_softmax_pv(s, bias, vals_a, vals_b)

    def finish_stage(pat, planes, per_plane, n, accs, maxes):
        lo = n * per_plane
        out, den, mx = _block_finish(accs, maxes, head_a)
        for j, r in enumerate(planes):
            rows = slice(j * per_plane, (j + 1) * per_plane)
            o_scr[pat, r, lo:lo + per_plane, :] = out[rows]
            d_scr[pat, r, lo:lo + per_plane, :] = den[rows]
            m_scr[pat, r, lo:lo + per_plane, :] = mx[rows]

    in_scores, in_softmax = {}, {}
    for step in range(len(blocks) + ATTN_SKEW_SCORES + ATTN_SKEW_FINISH):
        if step < len(blocks):
            in_scores[step] = scores_stage(*blocks[step])
        mid = step - ATTN_SKEW_SCORES
        if 0 <= mid < len(blocks):
            in_softmax[mid] = softmax_stage(*blocks[mid], *in_scores.pop(mid))
        last = mid - ATTN_SKEW_FINISH
        if 0 <= last < len(blocks):
            finish_stage(*blocks[last], *in_softmax.pop(last))

    for r in range(PLANES):
        m = jnp.maximum(jnp.maximum(m_scr[0, r], m_scr[1, r]), m_scr[2, r])
        num = den = None
        for pat in range(len(dilations)):
            w = jnp.exp2(m_scr[pat, r] - m)
            num = w * o_scr[pat, r] if num is None else num + w * o_scr[pat, r]
            den = w * d_scr[pat, r] if den is None else den + w * d_scr[pat, r]
        y_ref[0, 0, pl.ds(r, blk, stride=PLANES), :] = num / den


def _dilated_attention(q, k, v):
    batch, n_pair, planes, rows, lanes = q.shape
    blk = ATTN_BLOCK
    tile = planes * blk
    cur = pl.BlockSpec((1, 1, planes, blk, lanes), lambda b, h, t: (b, h, 0, t, 0))
    prev = pl.BlockSpec((1, 1, planes, blk, lanes), lambda b, h, t: (b, h, 0, jnp.maximum(t - 1, 0), 0))
    return pl.pallas_call(
        _attn_kernel,
        grid=(batch, n_pair, rows // blk),
        in_specs=[cur, prev, cur, prev, cur],
        out_specs=pl.BlockSpec((1, 1, tile, lanes), lambda b, h, t: (b, h, t, 0)),
        out_shape=jax.ShapeDtypeStruct((batch, n_pair, planes * rows, lanes), F32),
        scratch_shapes=[pltpu.VMEM((len(DILATED_PAIRS), planes, blk, lanes), F32)] * 3
        + [pltpu.VMEM((2, len(DILATED_PAIRS), blk, 2 * blk), F32)],
        compiler_params=pltpu.CompilerParams(
            dimension_semantics=("arbitrary",) * 3, vmem_limit_bytes=VMEM_LIMIT),
        name="dilated_attn",
    )(q, k, k, v, v)


def _hgrn_blocks(base, lb_all, nw_ref, hg_ref, y_ref, st_ref, b_scr, q_scr, k_scr, a_scr, tri, pair_key):
    t_blk = HGRN_BLOCK
    dk = HGRN_DK
    sub = HGRN_SUB
    w = HGRN_WIDTH
    rows = pl.ds(base, t_blk)
    heads = range(HGRN_HEADS)
    cols = [slice(h * dk, (h + 1) * dk) for h in heads]
    qf, key, v_bf, b, o = {}, {}, {}, {}, {}

    for h in heads:
        lb = lb_all[:, cols[h]]
        qr = hg_ref[rows, h * dk:(h + 1) * dk]
        fl = hg_ref[rows, w + h * dk:w + (h + 1) * dk]
        f = lb + (1.0 - lb) * _sigmoid(fl)
        lf = jnp.log2(f)
        key[h] = 1.0 - f
        qf[h] = qr * _sigmoid(qr)
        v_bf[h] = hg_ref[rows, 2 * w + h * dk:2 * w + (h + 1) * dk].astype(BF16)
        hi = lf.astype(BF16)
        r1 = lf - hi.astype(F32)
        mid = r1.astype(BF16)
        lo = (r1 - mid.astype(F32)).astype(BF16)
        b[h] = _dot(tri, hi) + _dot(tri, mid) + _dot(tri, lo)
        b_scr[h] = b[h]
        q_scr[h] = qf[h]
        k_scr[h] = jnp.log2(key[h]) - b[h]
        yield

    for h in heads:
        b_last = b_scr[h, t_blk - 1:t_blk, :]
        st = st_ref[h]
        q_inter = (qf[h] * jnp.exp2(b[h])).astype(BF16)
        o[h] = _dot_nt(q_inter, st.astype(BF16))
        k_upd = (key[h] * jnp.exp2(b_last - b[h])).astype(BF16)
        st_ref[h] = st * jnp.exp2(b_last) + _dot_tn(v_bf[h], k_upd)
    yield

    sub_id = lax.broadcasted_iota(jnp.int32, (sub, t_blk), 0)
    lane_id = lax.broadcasted_iota(jnp.int32, (sub, t_blk), 1)
    keep = [jnp.logical_and((lane_id & (sub - 1)) == s, sub_id >= s) for s in range(sub)]
    for c in range(t_blk // sub):
        lo_row = c * sub
        in_chunk = (lane_id & -sub) == lo_row
        for h in heads:
            b_t = b_scr[h, lo_row:lo_row + sub, :]
            q_t = q_scr[h, lo_row:lo_row + sub, :]
            blk_rows = jnp.zeros((sub, t_blk), F32)
            for s in range(sub):
                c_s = k_scr[h, lo_row + s:lo_row + s + 1, :]
                col = jnp.sum(q_t * jnp.exp2(b_t + c_s), axis=-1, keepdims=True)
                blk_rows = jnp.where(keep[s], col, blk_rows)
            a_scr[h, lo_row:lo_row + sub, :] = jnp.where(in_chunk, blk_rows, 0.0)
        if c % 2 == 1:
            yield

    a_mat = {h: a_scr[h] for h in heads}
    rid = lax.broadcasted_iota(jnp.int32, (t_blk, dk), 0)
    half = sub
    while half < t_blk:
        span = 2 * half
        upper = (rid & half) != 0
        for h in heads:
            refs = [jnp.broadcast_to(b_scr[h, p * span + half - 1:p * span + half, :], (span, dk))
                    for p in range(t_blk // span)]
            d = b[h] - (refs[0] if len(refs) == 1 else jnp.concatenate(refs, axis=0))
            x = (jnp.where(upper, qf[h], key[h]) * jnp.exp2(jnp.where(upper, d, -d))).astype(BF16)
            a_mat[h] = jnp.where(pair_key == half, _dot_nt(x, x), a_mat[h])
        half = span
        yield

    for h in heads:
        gr = hg_ref[rows, 3 * w + h * dk:3 * w + (h + 1) * dk]
        out = o[h] + _dot(a_mat[h].astype(BF16), v_bf[h])
        out = _rms_scale(out) * nw_ref[:, cols[h]]
        y_ref[rows, cols[h]] = (out * (gr * _sigmoid(gr))).astype(BF16)


def _in_proj_hgrn_kernel(x_ref, n1_ref, w_ref, cos_ref, sin_ref, lbl_ref, nw_ref,
                         q_ref, k_ref, v_ref, y_ref,
                         hg_scr, w_scr, stage_ref, st_ref, b_scr, q_scr, k_scr, a_scr, *, tiles_per_seq):
    t_blk = HGRN_BLOCK
    step = pl.program_id(0)
    write_slot = step & 1

    @pl.when(step == 0)
    def _():
        hg_scr[1] = jnp.zeros_like(hg_scr[1])
        for col in range(0, 2 * ATTN_WIDTH, LANES):
            w_scr[:, col:col + LANES] = _qk_lane_order(w_ref[0, :, col:col + LANES]).astype(BF16)
        for col in range(2 * ATTN_WIDTH, IN_PROJ_WIDTH, IN_PROJ_CHUNK):
            w_scr[:, col:col + IN_PROJ_CHUNK] = w_ref[0, :, col:col + IN_PROJ_CHUNK].astype(BF16)

    @pl.when((step == 0) | (step % tiles_per_seq == 1))
    def _():
        st_ref[...] = jnp.zeros_like(st_ref)

    u = (_rms_scale(x_ref[0]) * n1_ref[...]).astype(BF16)
    cos = cos_ref[...]
    sin = sin_ref[...]
    a = ATTN_WIDTH
    n_pair = a // LANES
    hg_out = hg_scr.at[write_slot]
    q_scale = HEAD_DIM ** -0.5 * float(np.log2(np.e))
    rows_per_plane = stage_ref.shape[1] // PLANES
    pieces = []

    def project(col, width):
        return _dot(u, w_scr[:, col:col + width])

    def attn_piece(i, j0, rotary, scale):
        def run():
            p = project(i * a + j0 * LANES, IN_PROJ_CHUNK)
            for jj in range(IN_PROJ_CHUNK // LANES):
                t = p[:, jj * LANES:(jj + 1) * LANES]
                if rotary:
                    t = _rotate_half_pairs(t, cos, sin)
                stage_ref[i * n_pair + j0 + jj] = t if scale is None else t * scale
        return run

    def plane_piece(i, out_ref, j):
        def run():
            for r in range(PLANES):
                out_ref[0, j, r] = stage_ref[i * n_pair + j, pl.ds(r, rows_per_plane, stride=PLANES), :]
        return run

    def hg_piece(c):
        def run():
            hg_out[:, c * IN_PROJ_CHUNK:(c + 1) * IN_PROJ_CHUNK] = project(3 * a + c * IN_PROJ_CHUNK,
                                                                           IN_PROJ_CHUNK)
        return run

    for i, (out_ref, rotary, scale) in enumerate(((q_ref, True, q_scale), (k_ref, True, None),
                                                   (v_ref, False, None))):
        for j0 in range(0, n_pair, IN_PROJ_CHUNK // LANES):
            pieces.append(attn_piece(i, j0, rotary, scale))
        pieces += [plane_piece(i, out_ref, j) for j in range(n_pair)]
    pieces += [hg_piece(c) for c in range(4 * HGRN_WIDTH // IN_PROJ_CHUNK)]

    lbl = lbl_ref[...]
    mx = jnp.max(lbl, axis=0, keepdims=True)
    e = jnp.exp(lbl - mx)
    lb_all = e[0:1, :] / jnp.sum(e, axis=0, keepdims=True)

    row = lax.broadcasted_iota(jnp.int32, (t_blk, t_blk), 0)
    col = lax.broadcasted_iota(jnp.int32, (t_blk, t_blk), 1)
    tri = (col <= row).astype(BF16)
    pair_key = jnp.zeros((t_blk, t_blk), jnp.int32)
    half = HGRN_SUB
    while half < t_blk:
        pair_key = jnp.where(jnp.logical_and((row ^ col) >= half, col < row), half, pair_key)
        half *= 2

    hg_in = hg_scr.at[1 - write_slot]
    y_out = y_ref.at[0]
    n_blocks = hg_scr.shape[1] // t_blk
    stages = [_hgrn_blocks(i * t_blk, lb_all, nw_ref, hg_in, y_out, st_ref, b_scr.at[i], q_scr.at[i],
                           k_scr.at[i], a_scr, tri, pair_key) for i in range(n_blocks)]
    for gen in stages:
        for _ in range(HGRN_HEADS):
            next(gen)
    for gen in stages:
        for _ in gen:
            if pieces:
                pieces.pop(0)()
    assert not pieces


def _in_proj_hgrn(x, norm1_w, w_in, cos_t, sin_t, lb_logits, hgrn_norm_w, tm=512):
    batch, seq, _ = x.shape
    n_pair = ATTN_WIDTH // LANES
    tiles_per_seq = seq // tm
    n_tiles = batch * tiles_per_seq

    def proj_tile(s):
        s = jnp.minimum(s, n_tiles - 1)
        return s // tiles_per_seq, s % tiles_per_seq

    def hgrn_tile(s):
        s = jnp.maximum(s - 1, 0)
        return s // tiles_per_seq, s % tiles_per_seq

    plane_spec = pl.BlockSpec((1, n_pair, PLANES, tm // PLANES, LANES),
                              lambda s: (proj_tile(s)[0], 0, 0, proj_tile(s)[1], 0))
    plane_shape = jax.ShapeDtypeStruct((batch, n_pair, PLANES, seq // PLANES, LANES), F32)
    tab_spec = pl.BlockSpec((tm, LANES), lambda s: (proj_tile(s)[1], 0))
    blk_scratch = pltpu.VMEM((tm // HGRN_BLOCK, HGRN_HEADS, HGRN_BLOCK, HGRN_DK), F32)
    return pl.pallas_call(
        functools.partial(_in_proj_hgrn_kernel, tiles_per_seq=tiles_per_seq),
        grid=(n_tiles + 1,),
        in_specs=[pl.BlockSpec((1, tm, D_MODEL), lambda s: (*proj_tile(s), 0)), _resident((1, D_MODEL)),
                  _resident(w_in.shape), tab_spec, tab_spec,
                  _resident(lb_logits.shape), _resident((1, HGRN_WIDTH))],
        out_specs=[plane_spec] * 3 + [pl.BlockSpec((1, tm, HGRN_WIDTH), lambda s: (*hgrn_tile(s), 0))],
        out_shape=[plane_shape] * 3 + [jax.ShapeDtypeStruct((batch, seq, HGRN_WIDTH), BF16)],
        scratch_shapes=[pltpu.VMEM((2, tm, 4 * HGRN_WIDTH), F32),
                        pltpu.VMEM((D_MODEL, IN_PROJ_WIDTH), BF16),
                        pltpu.VMEM((3 * n_pair, tm, LANES), F32),
                        pltpu.VMEM((HGRN_HEADS, HGRN_DK, HGRN_DK), F32),
                        blk_scratch, blk_scratch, blk_scratch,
                        pltpu.VMEM((HGRN_HEADS, HGRN_BLOCK, HGRN_BLOCK), F32)],
        compiler_params=pltpu.CompilerParams(
            dimension_semantics=("arbitrary",), vmem_limit_bytes=VMEM_LIMIT),
        name="in_proj_hgrn2",
    )(x, norm1_w, w_in, cos_t, sin_t, lb_logits, hgrn_norm_w)


def _out_ffn_kernel(ya_ref, yb_ref, x_ref, wo_ref, n2_ref, wgu_ref, wd_ref, nf_ref, out_ref, *, chunk):
    mixed = jnp.concatenate([ya_ref[0, j].astype(BF16) for j in range(ATTN_WIDTH // LANES)]
                            + [yb_ref[0]], axis=1)
    h = x_ref[0] + _dot(mixed, wo_ref[0].astype(BF16))
    u = (_rms_scale(h) * n2_ref[...]).astype(BF16)
    ffn = jnp.zeros_like(h)
    for c in range(FFN_HIDDEN // chunk):
        cols = slice(c * chunk, (c + 1) * chunk)
        up_cols = slice(FFN_HIDDEN + c * chunk, FFN_HIDDEN + (c + 1) * chunk)
        gate = _dot(u, wgu_ref[0, :, cols].astype(BF16))
        up = _dot(u, wgu_ref[0, :, up_cols].astype(BF16))
        act = (gate * _sigmoid(gate) * up).astype(BF16)
        ffn = ffn + _dot(act, wd_ref[0, cols, :].astype(BF16))
    out_ref[0] = _rms_scale(h + ffn) * nf_ref[...]


def _out_ffn(ya, yb, x, w_out, norm2_w, w_gate_up, w_down, final_w, tm=512, chunk=256):
    batch, seq, _ = x.shape
    n_pair = ATTN_WIDTH // LANES
    row_spec = lambda width: pl.BlockSpec((1, tm, width), lambda b, i: (b, i, 0))
    return pl.pallas_call(
        functools.partial(_out_ffn_kernel, chunk=chunk),
        grid=(batch, seq // tm),
        in_specs=[pl.BlockSpec((1, n_pair, tm, LANES), lambda b, i: (b, 0, i, 0)),
                  row_spec(HGRN_WIDTH), row_spec(D_MODEL),
                  _resident(w_out.shape), _resident((1, D_MODEL)), _resident(w_gate_up.shape),
                  _resident(w_down.shape), _resident((1, D_MODEL))],
        out_specs=row_spec(D_MODEL),
        out_shape=jax.ShapeDtypeStruct((batch, seq, D_MODEL), F32),
        compiler_params=pltpu.CompilerParams(
            dimension_semantics=("arbitrary", "arbitrary"), vmem_limit_bytes=VMEM_LIMIT),
        name="out_proj_ffn",
    )(ya, yb, x, w_out, norm2_w, w_gate_up, w_down, final_w)


def kernel(x, norm1_w, w_in, lb_logits, hgrn_norm_w, w_out, norm2_w, w_gate_up, w_down, final_norm_w):
    batch, seq, d_model = x.shape
    assert d_model == D_MODEL and norm1_w.shape[0] == 1
    assert all(w // d == ATTN_BLOCK for w, d in DILATED_PAIRS)
    assert tuple(d for _, d in DILATED_PAIRS) == (1, 4, 16) and seq % (PLANES * ATTN_BLOCK) == 0
    cos_t, sin_t = _rope_tables(seq)
    q, k, v, yb = _in_proj_hgrn(x, norm1_w, w_in, cos_t, sin_t, lb_logits, hgrn_norm_w)
    ya = _dilated_attention(q, k, v)
    return _out_ffn(ya, yb, x, w_out, norm2_w, w_gate_up, w_down, final_norm_w.reshape(1, d_model))
```

```python
import functools

import numpy as np
import jax
import jax.numpy as jnp
from jax import lax
from jax.experimental import pallas as pl
from jax.experimental.pallas import tpu as pltpu

D_MODEL = 1024
ATTN_WIDTH = 512
ATTN_HEADS = 8
HEAD_DIM = 64
DILATED_PAIRS = ((128, 1), (512, 4), (2048, 16))
ATTN_BLOCK = 128
ROPE_THETA = 10000.0
HGRN_WIDTH = 512
HGRN_DK = 128
HGRN_HEADS = 4
FFN_HIDDEN = 2816
IN_PROJ_WIDTH = 3 * ATTN_WIDTH + 4 * HGRN_WIDTH
NORM_EPS = 1e-6

LANES = 128
PLANES = 16
ATTN_SKEW_SCORES = 2
ATTN_SKEW_FINISH = 2
ROW_TILE = 512
IN_PROJ_CHUNK = 512
FFN_CHUNK = 256
HGRN_BLOCK = 128
HGRN_SUB = 8
VMEM_LIMIT = 56 * 1024 * 1024

F32 = jnp.float32
BF16 = jnp.bfloat16


def _dot(a, b):
    return jnp.dot(a, b, preferred_element_type=F32)


def _dot_nt(a, b):
    return lax.dot_general(a, b, (((1,), (1,)), ((), ())), preferred_element_type=F32)


def _dot_tn(a, b):
    return lax.dot_general(a, b, (((0,), (0,)), ((), ())), preferred_element_type=F32)


def _rms_scale(x):
    return x * lax.rsqrt(jnp.mean(x * x, axis=-1, keepdims=True) + NORM_EPS)


def _sigmoid(x):
    return 1.0 / (1.0 + jnp.exp(-x))


def _resident(shape):
    return pl.BlockSpec(shape, lambda *_: (0,) * len(shape), pipeline_mode=pl.Buffered(1))


def _rope_tables(seq):
    half = HEAD_DIM // 2
    inv_freq = ROPE_THETA ** (-np.arange(half, dtype=np.float64) / half)
    ang = np.arange(seq, dtype=np.float64)[:, None] * inv_freq[None, :]
    cos, sin = np.cos(ang), np.sin(ang)
    cos_t = np.concatenate([cos, cos, cos, cos], axis=1)
    sin_t = np.concatenate([-sin, sin, -sin, sin], axis=1)
    return jnp.asarray(cos_t, F32), jnp.asarray(sin_t, F32)


def _rotate_half_pairs(t, cos, sin):
    lane = lax.broadcasted_iota(jnp.int32, t.shape, 1)
    fwd = pltpu.roll(t, LANES - HEAD_DIM // 2, 1)
    bwd = pltpu.roll(t, HEAD_DIM // 2, 1)
    partner = jnp.where((lane & (HEAD_DIM // 2)) == 0, fwd, bwd)
    return t * cos + partner * sin


def _block_scores(q, keys, head_a):
    qb = q.astype(BF16)
    zero = jnp.zeros_like(qb)
    lhs = jnp.concatenate([jnp.where(head_a, qb, zero), jnp.where(head_a, zero, qb)], axis=0)
    return _dot_nt(lhs, keys)


def _block_softmax_pv(s, bias, vals_a, vals_b):
    blk = ATTN_BLOCK
    accs, maxes = [], []
    for half, vals in enumerate((vals_a, vals_b)):
        sh = s[half * blk:(half + 1) * blk] + bias
        m = jnp.max(sh, axis=-1, keepdims=True)
        accs.append(_dot(jnp.exp2(sh - m).astype(BF16), vals))
        maxes.append(m)
    return accs, maxes


def _block_finish(accs, maxes, head_a):
    out = jnp.where(head_a, accs[0], accs[1])
    den = pltpu.roll(jnp.where(head_a, accs[1], accs[0]), HEAD_DIM, 1)
    mx = jnp.where(head_a, maxes[0], maxes[1])
    return out, den, mx


def _block_row_index(dilation):
    per_plane = ATTN_BLOCK * dilation // PLANES
    shift = per_plane.bit_length() - 1
    return lambda i: ((i & (per_plane - 1)) * (PLANES // dilation)) + (i >> shift)


def _attn_kernel(q_ref, kp_ref, kc_ref, vp_ref, vc_ref, y_ref, o_scr, d_scr, m_scr, bias_scr):
    blk = ATTN_BLOCK
    dilations = (16, 4, 1)
    head_a = lax.broadcasted_iota(jnp.int32, (blk, LANES), 1) < HEAD_DIM

    @pl.when((pl.program_id(0) == 0) & (pl.program_id(1) == 0) & (pl.program_id(2) == 0))
    def _():
        row = lax.broadcasted_iota(jnp.int32, (blk, 2 * blk), 0)
        col = lax.broadcasted_iota(jnp.int32, (blk, 2 * blk), 1)
        in_prev = col < blk
        col = col & (blk - 1)
        for pat, dilation in enumerate(dilations):
            index_of_row = _block_row_index(dilation)
            delta = index_of_row(row) - index_of_row(col)
            for variant, prev_offset in enumerate((4 * blk, blk)):
                d = jnp.where(in_prev, delta + prev_offset, delta)
                bias_scr[variant, pat] = jnp.where(jnp.logical_and(d >= 0, d <= blk), 0.0, -jnp.inf)

    has_prev = (pl.program_id(2) > 0).astype(jnp.int32)

    def gather(ref, planes, lo, size):
        parts = [ref[0, 0, r, lo:lo + size, :] for r in planes]
        return parts[0] if len(parts) == 1 else jnp.concatenate(parts, axis=0)

    def keys_values(k_ref, v_ref, planes, lo, size):
        k = gather(k_ref, planes, lo, size).astype(BF16)
        v = gather(v_ref, planes, lo, size).astype(BF16)
        one = jnp.ones_like(v)
        return k, jnp.where(head_a, v, one), jnp.where(head_a, one, v)

    blocks = []
    for pat, dilation in enumerate(dilations):
        per_plane = blk * dilation // PLANES
        for res in range(dilation):
            planes = list(range(res, PLANES, dilation))
            blocks += [(pat, planes, per_plane, n) for n in range(blk // per_plane)]

    carried = {}

    def scores_stage(pat, planes, per_plane, n):
        lo = n * per_plane
        prev = keys_values(kp_ref, vp_ref, planes, blk - per_plane, per_plane) if n == 0 else carried["kv"]
        cur = carried["kv"] = keys_values(kc_ref, vc_ref, planes, lo, per_plane)
        keys, vals_a, vals_b = (jnp.concatenate([p, c], axis=0) for p, c in zip(prev, cur))
        return _block_scores(gather(q_ref, planes, lo, per_plane), keys, head_a), vals_a, vals_b

    def softmax_stage(pat, planes, per_plane, n, s, vals_a, vals_b):
        bias = bias_scr[has_prev, pat] if n == 0 else bias_scr[1, pat]
        return _block_softmax_pv(s, bias, vals_a, vals_b)

    def finish_stage(pat, planes, per_plane, n, accs, maxes):
        lo = n * per_plane
        out, den, mx = _block_finish(accs, maxes, head_a)
        for j, r in enumerate(planes):
            rows = slice(j * per_plane, (j + 1) * per_plane)
            o_scr[pat, r, lo:lo + per_plane, :] = out[rows]
            d_scr[pat, r, lo:lo + per_plane, :] = den[rows]
            m_scr[pat, r, lo:lo + per_plane, :] = mx[rows]

    in_scores, in_softmax = {}, {}
    for step in range(len(blocks) + ATTN_SKEW_SCORES + ATTN_SKEW_FINISH):
        if step < len(blocks):
            in_scores[step] = scores_stage(*blocks[step])
        mid = step - ATTN_SKEW_SCORES
        if 0 <= mid < len(blocks):
            in_softmax[mid] = softmax_stage(*blocks[mid], *in_scores.pop(mid))
        last = mid - ATTN_SKEW_FINISH
        if 0 <= last < len(blocks):
            finish_stage(*blocks[last], *in_softmax.pop(last))

    for r in range(PLANES):
        m = jnp.maximum(jnp.maximum(m_scr[0, r], m_scr[1, r]), m_scr[2, r])
        num = den = None
        for pat in range(len(dilations)):
            w = jnp.exp2(m_scr[pat, r] - m)
            num = w * o_scr[pat, r] if num is None else num + w * o_scr[pat, r]
            den = w * d_scr[pat, r] if den is None else den + w * d_scr[pat, r]
        y_ref[0, 0, pl.ds(r, blk, stride=PLANES), :] = num / den


def _dilated_attention(q, k, v):
    batch, n_pair, planes, rows, lanes = q.shape
    blk = ATTN_BLOCK
    tile = planes * blk
    cur = pl.BlockSpec((1, 1, planes, blk, lanes), lambda b, h, t: (b, h, 0, t, 0))
    prev = pl.BlockSpec((1, 1, planes, blk, lanes), lambda b, h, t: (b, h, 0, jnp.maximum(t - 1, 0), 0))
    return pl.pallas_call(
        _attn_kernel,
        grid=(batch, n_pair, rows // blk),
        in_specs=[cur, prev, cur, prev, cur],
        out_specs=pl.BlockSpec((1, 1, tile, lanes), lambda b, h, t: (b, h, t, 0)),
        out_shape=jax.ShapeDtypeStruct((batch, n_pair, planes * rows, lanes), F32),
        scratch_shapes=[pltpu.VMEM((len(DILATED_PAIRS), planes, blk, lanes), F32)] * 3
        + [pltpu.VMEM((2, len(DILATED_PAIRS), blk, 2 * blk), F32)],
        compiler_params=pltpu.CompilerParams(
            dimension_semantics=("arbitrary",) * 3, vmem_limit_bytes=VMEM_LIMIT),
        name="dilated_attn",
    )(q, k, k, v, v)


def _hgrn_blocks(base, lb_all, nw_ref, hg_ref, y_ref, st_ref, b_scr, q_scr, k_scr, a_scr, tri, pair_key):
    t_blk = HGRN_BLOCK
    dk = HGRN_DK
    sub = HGRN_SUB
    w = HGRN_WIDTH
    rows = pl.ds(base, t_blk)
    heads = range(HGRN_HEADS)
    cols = [slice(h * dk, (h + 1) * dk) for h in heads]
    qf, key, v_bf, b, o = {}, {}, {}, {}, {}

    for h in heads:
        lb = lb_all[:, cols[h]]
        qr = hg_ref[rows, h * dk:(h + 1) * dk]
        fl = hg_ref[rows, w + h * dk:w + (h + 1) * dk]
        f = lb + (1.0 - lb) * _sigmoid(fl)
        lf = jnp.log2(f)
        key[h] = 1.0 - f
        qf[h] = qr * _sigmoid(qr)
        v_bf[h] = hg_ref[rows, 2 * w + h * dk:2 * w + (h + 1) * dk].astype(BF16)
        hi = lf.astype(BF16)
        r1 = lf - hi.astype(F32)
        mid = r1.astype(BF16)
        lo = (r1 - mid.astype(F32)).astype(BF16)
        b[h] = _dot(tri, hi) + _dot(tri, mid) + _dot(tri, lo)
        b_scr[h] = b[h]
        q_scr[h] = qf[h]
        k_scr[h] = jnp.log2(key[h]) - b[h]
        yield

    for h in heads:
        b_last = b_scr[h, t_blk - 1:t_blk, :]
        st = st_ref[h]
        q_inter = (qf[h] * jnp.exp2(b[h])).astype(BF16)
        o[h] = _dot_nt(q_inter, st.astype(BF16))
        k_upd = (key[h] * jnp.exp2(b_last - b[h])).astype(BF16)
        st_ref[h] = st * jnp.exp2(b_last) + _dot_tn(v_bf[h], k_upd)
    yield

    sub_id = lax.broadcasted_iota(jnp.int32, (sub, t_blk), 0)
    lane_id = lax.broadcasted_iota(jnp.int32, (sub, t_blk), 1)
    keep = [jnp.logical_and((lane_id & (sub - 1)) == s, sub_id >= s) for s in range(sub)]
    for c in range(t_blk // sub):
        lo_row = c * sub
        in_chunk = (lane_id & -sub) == lo_row
        for h in heads:
            b_t = b_scr[h, lo_row:lo_row + sub, :]
            q_t = q_scr[h, lo_row:lo_row + sub, :]
            blk_rows = jnp.zeros((sub, t_blk), F32)
            for s in range(sub):
                c_s = k_scr[h, lo_row + s:lo_row + s + 1, :]
                col = jnp.sum(q_t * jnp.exp2(b_t + c_s), axis=-1, keepdims=True)
                blk_rows = jnp.where(keep[s], col, blk_rows)
            a_scr[h, lo_row:lo_row + sub, :] = jnp.where(in_chunk, blk_rows, 0.0)
        if c % 2 == 1:
            yield

    a_mat = {h: a_scr[h] for h in heads}
    rid = lax.broadcasted_iota(jnp.int32, (t_blk, dk), 0)
    half = sub
    while half < t_blk:
        span = 2 * half
        upper = (rid & half) != 0
        for h in heads:
            refs = [jnp.broadcast_to(b_scr[h, p * span + half - 1:p * span + half, :], (span, dk))
                    for p in range(t_blk // span)]
            d = b[h] - (refs[0] if len(refs) == 1 else jnp.concatenate(refs, axis=0))
            x = (jnp.where(upper, qf[h], key[h]) * jnp.exp2(jnp.where(upper, d, -d))).astype(BF16)
            a_mat[h] = jnp.where(pair_key == half, _dot_nt(x, x), a_mat[h])
        half = span
        yield

    for h in heads:
        gr = hg_ref[rows, 3 * w + h * dk:3 * w + (h + 1) * dk]
        out = o[h] + _dot(a_mat[h].astype(BF16), v_bf[h])
        out = _rms_scale(out) * nw_ref[:, cols[h]]
        y_ref[rows, cols[h]] = (out * (gr * _sigmoid(gr))).astype(BF16)


def _in_proj_hgrn_kernel(x_ref, n1_ref, w_ref, cos_ref, sin_ref, lbl_ref, nw_ref,
                         q_ref, k_ref, v_ref, y_ref,
                         hg_scr, w_scr, stage_ref, st_ref, b_scr, q_scr, k_scr, a_scr, *, tiles_per_seq):
    t_blk = HGRN_BLOCK
    step = pl.program_id(0)
    write_slot = step & 1

    @pl.when(step == 0)
    def _():
        hg_scr[1] = jnp.zeros_like(hg_scr[1])
        for col in range(0, IN_PROJ_WIDTH, IN_PROJ_CHUNK):
            w_scr[:, col:col + IN_PROJ_CHUNK] = w_ref[0, :, col:col + IN_PROJ_CHUNK].astype(BF16)

    @pl.when((step == 0) | (step % tiles_per_seq == 1))
    def _():
        st_ref[...] = jnp.zeros_like(st_ref)

    u = (_rms_scale(x_ref[0]) * n1_ref[...]).astype(BF16)
    cos = cos_ref[...]
    sin = sin_ref[...]
    a = ATTN_WIDTH
    n_pair = a // LANES
    hg_out = hg_scr.at[write_slot]
    q_scale = HEAD_DIM ** -0.5 * float(np.log2(np.e))
    rows_per_plane = stage_ref.shape[1] // PLANES
    pieces = []

    def project(col, width):
        return _dot(u, w_scr[:, col:col + width])

    def attn_piece(i, j0, rotary, scale):
        def run():
            p = project(i * a + j0 * LANES, IN_PROJ_CHUNK)
            for jj in range(IN_PROJ_CHUNK // LANES):
                t = p[:, jj * LANES:(jj + 1) * LANES]
                if rotary:
                    t = _rotate_half_pairs(t, cos, sin)
                stage_ref[i * n_pair + j0 + jj] = t if scale is None else t * scale
        return run

    def plane_piece(i, out_ref, j):
        def run():
            for r in range(PLANES):
                out_ref[0, j, r] = stage_ref[i * n_pair + j, pl.ds(r, rows_per_plane, stride=PLANES), :]
        return run

    def hg_piece(c):
        def run():
            hg_out[:, c * IN_PROJ_CHUNK:(c + 1) * IN_PROJ_CHUNK] = project(3 * a + c * IN_PROJ_CHUNK,
                                                                           IN_PROJ_CHUNK)
        return run

    for i, (out_ref, rotary, scale) in enumerate(((q_ref, True, q_scale), (k_ref, True, None),
                                                   (v_ref, False, None))):
        for j0 in range(0, n_pair, IN_PROJ_CHUNK // LANES):
            pieces.append(attn_piece(i, j0, rotary, scale))
        pieces += [plane_piece(i, out_ref, j) for j in range(n_pair)]
    pieces += [hg_piece(c) for c in range(4 * HGRN_WIDTH // IN_PROJ_CHUNK)]

    lbl = lbl_ref[...]
    mx = jnp.max(lbl, axis=0, keepdims=True)
    e = jnp.exp(lbl - mx)
    lb_all = e[0:1, :] / jnp.sum(e, axis=0, keepdims=True)

    row = lax.broadcasted_iota(jnp.int32, (t_blk, t_blk), 0)
    col = lax.broadcasted_iota(jnp.int32, (t_blk, t_blk), 1)
    tri = (col <= row).astype(BF16)
    pair_key = jnp.zeros((t_blk, t_blk), jnp.int32)
    half = HGRN_SUB
    while half < t_blk:
        pair_key = jnp.where(jnp.logical_and((row ^ col) >= half, col < row), half, pair_key)
        half *= 2

    hg_in = hg_scr.at[1 - write_slot]
    y_out = y_ref.at[0]
    n_blocks = hg_scr.shape[1] // t_blk
    stages = [_hgrn_blocks(i * t_blk, lb_all, nw_ref, hg_in, y_out, st_ref, b_scr, q_scr, k_scr, a_scr,
                           tri, pair_key) for i in range(n_blocks)]
    for gen in stages:
        for _ in gen:
            if pieces:
                pieces.pop(0)()
    assert not pieces


def _in_proj_hgrn(x, norm1_w, w_in, cos_t, sin_t, lb_logits, hgrn_norm_w):
    batch, seq, _ = x.shape
    tm = ROW_TILE
    n_pair = ATTN_WIDTH // LANES
    tiles_per_seq = seq // tm
    n_tiles = batch * tiles_per_seq

    def proj_tile(s):
        s = jnp.minimum(s, n_tiles - 1)
        return s // tiles_per_seq, s % tiles_per_seq

    def hgrn_tile(s):
        s = jnp.maximum(s - 1, 0)
        return s // tiles_per_seq, s % tiles_per_seq

    plane_spec = pl.BlockSpec((1, n_pair, PLANES, tm // PLANES, LANES),
                              lambda s: (proj_tile(s)[0], 0, 0, proj_tile(s)[1], 0))
    plane_shape = jax.ShapeDtypeStruct((batch, n_pair, PLANES, seq // PLANES, LANES), F32)
    tab_spec = pl.BlockSpec((tm, LANES), lambda s: (proj_tile(s)[1], 0))
    blk_scratch = pltpu.VMEM((HGRN_HEADS, HGRN_BLOCK, HGRN_DK), F32)
    return pl.pallas_call(
        functools.partial(_in_proj_hgrn_kernel, tiles_per_seq=tiles_per_seq),
        grid=(n_tiles + 1,),
        in_specs=[pl.BlockSpec((1, tm, D_MODEL), lambda s: (*proj_tile(s), 0)), _resident((1, D_MODEL)),
                  _resident(w_in.shape), tab_spec, tab_spec,
                  _resident(lb_logits.shape), _resident((1, HGRN_WIDTH))],
        out_specs=[plane_spec] * 3 + [pl.BlockSpec((1, tm, HGRN_WIDTH), lambda s: (*hgrn_tile(s), 0))],
        out_shape=[plane_shape] * 3 + [jax.ShapeDtypeStruct((batch, seq, HGRN_WIDTH), BF16)],
        scratch_shapes=[pltpu.VMEM((2, tm, 4 * HGRN_WIDTH), F32),
                        pltpu.VMEM((D_MODEL, IN_PROJ_WIDTH), BF16),
                        pltpu.VMEM((3 * n_pair, tm, LANES), F32),
                        pltpu.VMEM((HGRN_HEADS, HGRN_DK, HGRN_DK), F32),
                        blk_scratch, blk_scratch, blk_scratch,
                        pltpu.VMEM((HGRN_HEADS, HGRN_BLOCK, HGRN_BLOCK), F32)],
        compiler_params=pltpu.CompilerParams(
            dimension_semantics=("arbitrary",), vmem_limit_bytes=VMEM_LIMIT),
        name="in_proj_hgrn2",
    )(x, norm1_w, w_in, cos_t, sin_t, lb_logits, hgrn_norm_w)


def _out_ffn_kernel(ya_ref, yb_ref, x_ref, wo_ref, n2_ref, wgu_ref, wd_ref, nf_ref, out_ref):
    chunk = FFN_CHUNK
    mixed = jnp.concatenate([ya_ref[0, j].astype(BF16) for j in range(ATTN_WIDTH // LANES)]
                            + [yb_ref[0]], axis=1)
    h = x_ref[0] + _dot(mixed, wo_ref[0].astype(BF16))
    u = (_rms_scale(h) * n2_ref[...]).astype(BF16)
    ffn = jnp.zeros_like(h)
    for c in range(FFN_HIDDEN // chunk):
        cols = slice(c * chunk, (c + 1) * chunk)
        up_cols = slice(FFN_HIDDEN + c * chunk, FFN_HIDDEN + (c + 1) * chunk)
        gate = _dot(u, wgu_ref[0, :, cols].astype(BF16))
        up = _dot(u, wgu_ref[0, :, up_cols].astype(BF16))
        act = (gate * _sigmoid(gate) * up).astype(BF16)
        ffn = ffn + _dot(act, wd_ref[0, cols, :].astype(BF16))
    out_ref[0] = _rms_scale(h + ffn) * nf_ref[...]


def _out_ffn(ya, yb, x, w_out, norm2_w, w_gate_up, w_down, final_w):
    batch, seq, _ = x.shape
    tm = ROW_TILE
    n_pair = ATTN_WIDTH // LANES
    row_spec = lambda width: pl.BlockSpec((1, tm, width), lambda b, i: (b, i, 0))
    return pl.pallas_call(
        _out_ffn_kernel,
        grid=(batch, seq // tm),
        in_specs=[pl.BlockSpec((1, n_pair, tm, LANES), lambda b, i: (b, 0, i, 0)),
                  row_spec(HGRN_WIDTH), row_spec(D_MODEL),
                  _resident(w_out.shape), _resident((1, D_MODEL)), _resident(w_gate_up.shape),
                  _resident(w_down.shape), _resident((1, D_MODEL))],
        out_specs=row_spec(D_MODEL),
        out_shape=jax.ShapeDtypeStruct((batch, seq, D_MODEL), F32),
        compiler_params=pltpu.CompilerParams(
            dimension_semantics=("arbitrary", "arbitrary"), vmem_limit_bytes=VMEM_LIMIT),
        name="out_proj_ffn",
    )(ya, yb, x, w_out, norm2_w, w_gate_up, w_down, final_w)


def kernel(x, norm1_w, w_in, lb_logits, hgrn_norm_w, w_out, norm2_w, w_gate_up, w_down, final_norm_w):
    batch, seq, d_model = x.shape
    assert d_model == D_MODEL and norm1_w.shape[0] == 1
    assert all(w // d == ATTN_BLOCK for w, d in DILATED_PAIRS)
    assert tuple(d for _, d in DILATED_PAIRS) == (1, 4, 16) and seq % (PLANES * ATTN_BLOCK) == 0
    cos_t, sin_t = _rope_tables(seq)
    q, k, v, yb = _in_proj_hgrn(x, norm1_w, w_in, cos_t, sin_t, lb_logits, hgrn_norm_w)
    ya = _dilated_attention(q, k, v)
    return _out_ffn(ya, yb, x, w_out, norm2_w, w_gate_up, w_down, final_norm_w.reshape(1, d_model))
```

```python
import functools

import numpy as np
import jax
import jax.numpy as jnp
from jax import lax
from jax.experimental import pallas as pl
from jax.experimental.pallas import tpu as pltpu

D_MODEL = 1024
ATTN_WIDTH = 512
ATTN_HEADS = 8
HEAD_DIM = 64
DILATED_PAIRS = ((128, 1), (512, 4), (2048, 16))
ATTN_BLOCK = 128
ROPE_THETA = 10000.0
HGRN_WIDTH = 512
HGRN_DK = 128
HGRN_HEADS = 4
FFN_HIDDEN = 2816
IN_PROJ_WIDTH = 3 * ATTN_WIDTH + 4 * HGRN_WIDTH
NORM_EPS = 1e-6

LANES = 128
PLANES = 16
ATTN_SKEW_SCORES = 2
ATTN_SKEW_FINISH = 2
ROW_TILE = 512
IN_PROJ_CHUNK = 512
FFN_CHUNK = 256
HGRN_BLOCK = 128
HGRN_SUB = 8
VMEM_LIMIT = 56 * 1024 * 1024

F32 = jnp.float32
BF16 = jnp.bfloat16


def _dot(a, b):
    return jnp.dot(a, b, preferred_element_type=F32)


def _dot_nt(a, b):
    return lax.dot_general(a, b, (((1,), (1,)), ((), ())), preferred_element_type=F32)


def _dot_tn(a, b):
    return lax.dot_general(a, b, (((0,), (0,)), ((), ())), preferred_element_type=F32)


def _rms_scale(x):
    return x * lax.rsqrt(jnp.mean(x * x, axis=-1, keepdims=True) + NORM_EPS)


def _sigmoid(x):
    return 1.0 / (1.0 + jnp.exp(-x))


def _resident(shape):
    return pl.BlockSpec(shape, lambda *_: (0,) * len(shape), pipeline_mode=pl.Buffered(1))


def _rope_tables(seq):
    half = HEAD_DIM // 2
    inv_freq = ROPE_THETA ** (-np.arange(half, dtype=np.float64) / half)
    ang = np.arange(seq, dtype=np.float64)[:, None] * inv_freq[None, :]
    cos, sin = np.cos(ang), np.sin(ang)
    cos_t = np.concatenate([cos, cos, cos, cos], axis=1)
    sin_t = np.concatenate([-sin, sin, -sin, sin], axis=1)
    return jnp.asarray(cos_t, F32), jnp.asarray(sin_t, F32)


def _rotate_half_pairs(t, cos, sin):
    lane = lax.broadcasted_iota(jnp.int32, t.shape, 1)
    fwd = pltpu.roll(t, LANES - HEAD_DIM // 2, 1)
    bwd = pltpu.roll(t, HEAD_DIM // 2, 1)
    partner = jnp.where((lane & (HEAD_DIM // 2)) == 0, fwd, bwd)
    return t * cos + partner * sin


def _block_scores(q, keys, head_a):
    qb = q.astype(BF16)
    zero = jnp.zeros_like(qb)
    lhs = jnp.concatenate([jnp.where(head_a, qb, zero), jnp.where(head_a, zero, qb)], axis=0)
    return _dot_nt(lhs, keys)


def _block_softmax_pv(s, bias, vals_a, vals_b):
    blk = ATTN_BLOCK
    accs, maxes = [], []
    for half, vals in enumerate((vals_a, vals_b)):
        sh = s[half * blk:(half + 1) * blk] + bias
        m = jnp.max(sh, axis=-1, keepdims=True)
        accs.append(_dot(jnp.exp2(sh - m).astype(BF16), vals))
        maxes.append(m)
    return accs, maxes


def _block_finish(accs, maxes, head_a):
    out = jnp.where(head_a, accs[0], accs[1])
    den = pltpu.roll(jnp.where(head_a, accs[1], accs[0]), HEAD_DIM, 1)
    mx = jnp.where(head_a, maxes[0], maxes[1])
    return out, den, mx


def _block_row_index(dilation):
    per_plane = ATTN_BLOCK * dilation // PLANES
    shift = per_plane.bit_length() - 1
    return lambda i: ((i & (per_plane - 1)) * (PLANES // dilation)) + (i >> shift)


def _attn_kernel(q_ref, kp_ref, kc_ref, vp_ref, vc_ref, y_ref, o_scr, d_scr, m_scr, bias_scr):
    blk = ATTN_BLOCK
    dilations = (16, 4, 1)
    head_a = lax.broadcasted_iota(jnp.int32, (blk, LANES), 1) < HEAD_DIM

    @pl.when((pl.program_id(0) == 0) & (pl.program_id(1) == 0) & (pl.program_id(2) == 0))
    def _():
        row = lax.broadcasted_iota(jnp.int32, (blk, 2 * blk), 0)
        col = lax.broadcasted_iota(jnp.int32, (blk, 2 * blk), 1)
        in_prev = col < blk
        col = col & (blk - 1)
        for pat, dilation in enumerate(dilations):
            index_of_row = _block_row_index(dilation)
            delta = index_of_row(row) - index_of_row(col)
            for variant, prev_offset in enumerate((4 * blk, blk)):
                d = jnp.where(in_prev, delta + prev_offset, delta)
                bias_scr[variant, pat] = jnp.where(jnp.logical_and(d >= 0, d <= blk), 0.0, -jnp.inf)

    has_prev = (pl.program_id(2) > 0).astype(jnp.int32)

    def gather(ref, planes, lo, size):
        parts = [ref[0, 0, r, lo:lo + size, :] for r in planes]
        return parts[0] if len(parts) == 1 else jnp.concatenate(parts, axis=0)

    def keys_values(k_ref, v_ref, planes, lo, size):
        k = gather(k_ref, planes, lo, size).astype(BF16)
        v = gather(v_ref, planes, lo, size).astype(BF16)
        one = jnp.ones_like(v)
        return k, jnp.where(head_a, v, one), jnp.where(head_a, one, v)

    blocks = []
    for pat, dilation in enumerate(dilations):
        per_plane = blk * dilation // PLANES
        for res in range(dilation):
            planes = list(range(res, PLANES, dilation))
            blocks += [(pat, planes, per_plane, n) for n in range(blk // per_plane)]

    carried = {}

    def scores_stage(pat, planes, per_plane, n):
        lo = n * per_plane
        prev = keys_values(kp_ref, vp_ref, planes, blk - per_plane, per_plane) if n == 0 else carried["kv"]
        cur = carried["kv"] = keys_values(kc_ref, vc_ref, planes, lo, per_plane)
        keys, vals_a, vals_b = (jnp.concatenate([p, c], axis=0) for p, c in zip(prev, cur))
        return _block_scores(gather(q_ref, planes, lo, per_plane), keys, head_a), vals_a, vals_b

    def softmax_stage(pat, planes, per_plane, n, s, vals_a, vals_b):
        bias = bias_scr[has_prev, pat] if n == 0 else bias_scr[1, pat]
        return _block_softmax_pv(s, bias, vals_a, vals_b)

    def finish_stage(pat, planes, per_plane, n, accs, maxes):
        lo = n * per_plane
        out, den, mx = _block_finish(accs, maxes, head_a)
        for j, r in enumerate(planes):
            rows = slice(j * per_plane, (j + 1) * per_plane)
            o_scr[pat, r, lo:lo + per_plane, :] = out[rows]
            d_scr[pat, r, lo:lo + per_plane, :] = den[rows]
            m_scr[pat, r, lo:lo + per_plane, :] = mx[rows]

    in_scores, in_softmax = {}, {}
    for step in range(len(blocks) + ATTN_SKEW_SCORES + ATTN_SKEW_FINISH):
        if step < len(blocks):
            in_scores[step] = scores_stage(*blocks[step])
        mid = step - ATTN_SKEW_SCORES
        if 0 <= mid < len(blocks):
            in_softmax[mid] = softmax_stage(*blocks[mid], *in_scores.pop(mid))
        last = mid - ATTN_SKEW_FINISH
        if 0 <= last < len(blocks):
            finish_stage(*blocks[last], *in_softmax.pop(last))

    for r in range(PLANES):
        m = jnp.maximum(jnp.maximum(m_scr[0, r], m_scr[1, r]), m_scr[2, r])
        num = den = None
        for pat in range(len(dilations)):
            w = jnp.exp2(m_scr[pat, r] - m)
            num = w * o_scr[pat, r] if num is None else num + w * o_scr[pat, r]
            den = w * d_scr[pat, r] if den is None else den + w * d_scr[pat, r]
        y_ref[0, 0, pl.ds(r, blk, stride=PLANES), :] = num / den


def _dilated_attention(q, k, v):
    batch, n_pair, planes, rows, lanes = q.shape
    blk = ATTN_BLOCK
    tile = planes * blk
    cur = pl.BlockSpec((1, 1, planes, blk, lanes), lambda b, h, t: (b, h, 0, t, 0))
    prev = pl.BlockSpec((1, 1, planes, blk, lanes), lambda b, h, t: (b, h, 0, jnp.maximum(t - 1, 0), 0))
    return pl.pallas_call(
        _attn_kernel,
        grid=(batch, n_pair, rows // blk),
        in_specs=[cur, prev, cur, prev, cur],
        out_specs=pl.BlockSpec((1, 1, tile, lanes), lambda b, h, t: (b, h, t, 0)),
        out_shape=jax.ShapeDtypeStruct((batch, n_pair, planes * rows, lanes), F32),
        scratch_shapes=[pltpu.VMEM((len(DILATED_PAIRS), planes, blk, lanes), F32)] * 3
        + [pltpu.VMEM((2, len(DILATED_PAIRS), blk, 2 * blk), F32)],
        compiler_params=pltpu.CompilerParams(
            dimension_semantics=("arbitrary",) * 3, vmem_limit_bytes=VMEM_LIMIT),
        name="dilated_attn",
    )(q, k, k, v, v)


def _hgrn_blocks(base, lb_all, nw_ref, hg_ref, y_ref, st_ref, b_scr, q_scr, k_scr, a_scr, tri, pair_key):
    t_blk = HGRN_BLOCK
    dk = HGRN_DK
    sub = HGRN_SUB
    w = HGRN_WIDTH
    rows = pl.ds(base, t_blk)
    heads = range(HGRN_HEADS)
    cols = [slice(h * dk, (h + 1) * dk) for h in heads]
    qf, key, v_bf, b, o = {}, {}, {}, {}, {}

    for h in heads:
        lb = lb_all[:, cols[h]]
        qr = hg_ref[rows, h * dk:(h + 1) * dk]
        fl = hg_ref[rows, w + h * dk:w + (h + 1) * dk]
        f = lb + (1.0 - lb) * _sigmoid(fl)
        lf = jnp.log2(f)
        key[h] = 1.0 - f
        qf[h] = qr * _sigmoid(qr)
        v_bf[h] = hg_ref[rows, 2 * w + h * dk:2 * w + (h + 1) * dk].astype(BF16)
        hi = lf.astype(BF16)
        r1 = lf - hi.astype(F32)
        mid = r1.astype(BF16)
        lo = (r1 - mid.astype(F32)).astype(BF16)
        b[h] = _dot(tri, hi) + _dot(tri, mid) + _dot(tri, lo)
        b_scr[h] = b[h]
        q_scr[h] = qf[h]
        k_scr[h] = jnp.log2(key[h]) - b[h]
        yield

    for h in heads:
        b_last = b_scr[h, t_blk - 1:t_blk, :]
        st = st_ref[h]
        q_inter = (qf[h] * jnp.exp2(b[h])).astype(BF16)
        o[h] = _dot_nt(q_inter, st.astype(BF16))
        k_upd = (key[h] * jnp.exp2(b_last - b[h])).astype(BF16)
        st_ref[h] = st * jnp.exp2(b_last) + _dot_tn(v_bf[h], k_upd)
    yield

    sub_id = lax.broadcasted_iota(jnp.int32, (sub, t_blk), 0)
    lane_id = lax.broadcasted_iota(jnp.int32, (sub, t_blk), 1)
    keep = [jnp.logical_and((lane_id & (sub - 1)) == s, sub_id >= s) for s in range(sub)]
    for c in range(t_blk // sub):
        lo_row = c * sub
        in_chunk = (lane_id & -sub) == lo_row
        for h in heads:
            b_t = b_scr[h, lo_row:lo_row + sub, :]
            q_t = q_scr[h, lo_row:lo_row + sub, :]
            blk_rows = jnp.zeros((sub, t_blk), F32)
            for s in range(sub):
                c_s = k_scr[h, lo_row + s:lo_row + s + 1, :]
                col = jnp.sum(q_t * jnp.exp2(b_t + c_s), axis=-1, keepdims=True)
                blk_rows = jnp.where(keep[s], col, blk_rows)
            a_scr[h, lo_row:lo_row + sub, :] = jnp.where(in_chunk, blk_rows, 0.0)
        if c % 2 == 1:
            yield

    a_mat = {h: a_scr[h] for h in heads}
    rid = lax.broadcasted_iota(jnp.int32, (t_blk, dk), 0)
    half = sub
    while half < t_blk:
        span = 2 * half
        upper = (rid & half) != 0
        for h in heads:
            refs = [jnp.broadcast_to(b_scr[h, p * span + half - 1:p * span + half, :], (span, dk))
                    for p in range(t_blk // span)]
            d = b[h] - (refs[0] if len(refs) == 1 else jnp.concatenate(refs, axis=0))
            x = (jnp.where(upper, qf[h], key[h]) * jnp.exp2(jnp.where(upper, d, -d))).astype(BF16)
            a_mat[h] = jnp.where(pair_key == half, _dot_nt(x, x), a_mat[h])
        half = span
        yield

    for h in heads:
        gr = hg_ref[rows, 3 * w + h * dk:3 * w + (h + 1) * dk]
        out = o[h] + _dot(a_mat[h].astype(BF16), v_bf[h])
        out = _rms_scale(out) * nw_ref[:, cols[h]]
        y_ref[rows, cols[h]] = (out * (gr * _sigmoid(gr))).astype(BF16)


def _in_proj_hgrn_kernel(x_ref, n1_ref, w_ref, *refs, tiles_per_seq):
    step = pl.program_id(0)
    last = pl.num_programs(0) - 1
    w_scr, st_ref = refs[9], refs[11]

    @pl.when(step == 0)
    def _():
        for col in range(0, IN_PROJ_WIDTH, IN_PROJ_CHUNK):
            w_scr[:, col:col + IN_PROJ_CHUNK] = w_ref[0, :, col:col + IN_PROJ_CHUNK].astype(BF16)
        _in_proj_hgrn_step(True, False, x_ref, n1_ref, *refs)

    @pl.when(step % tiles_per_seq == 1)
    def _():
        st_ref[...] = jnp.zeros_like(st_ref)

    @pl.when((step > 0) & (step < last))
    def _():
        _in_proj_hgrn_step(True, True, x_ref, n1_ref, *refs)

    @pl.when(step == last)
    def _():
        _in_proj_hgrn_step(False, True, x_ref, n1_ref, *refs)


def _in_proj_hgrn_step(do_proj, do_hgrn, x_ref, n1_ref, cos_ref, sin_ref, lbl_ref, nw_ref,
                       q_ref, k_ref, v_ref, y_ref,
                       hg_scr, w_scr, stage_ref, st_ref, b_scr, q_scr, k_scr, a_scr):
    t_blk = HGRN_BLOCK
    step = pl.program_id(0)
    write_slot = step & 1

    u = (_rms_scale(x_ref[0]) * n1_ref[...]).astype(BF16)
    cos = cos_ref[...]
    sin = sin_ref[...]
    a = ATTN_WIDTH
    n_pair = a // LANES
    hg_out = hg_scr.at[write_slot]
    q_scale = HEAD_DIM ** -0.5 * float(np.log2(np.e))
    rows_per_plane = stage_ref.shape[1] // PLANES
    pieces = []

    def project(col, width):
        return _dot(u, w_scr[:, col:col + width])

    def attn_piece(i, j0, rotary, scale):
        def run():
            p = project(i * a + j0 * LANES, IN_PROJ_CHUNK)
            for jj in range(IN_PROJ_CHUNK // LANES):
                t = p[:, jj * LANES:(jj + 1) * LANES]
                if rotary:
                    t = _rotate_half_pairs(t, cos, sin)
                stage_ref[i * n_pair + j0 + jj] = t if scale is None else t * scale
        return run

    def plane_piece(i, out_ref, j):
        def run():
            for r in range(PLANES):
                out_ref[0, j, r] = stage_ref[i * n_pair + j, pl.ds(r, rows_per_plane, stride=PLANES), :]
        return run

    def hg_piece(c):
        def run():
            hg_out[:, c * IN_PROJ_CHUNK:(c + 1) * IN_PROJ_CHUNK] = project(3 * a + c * IN_PROJ_CHUNK,
                                                                           IN_PROJ_CHUNK)
        return run

    for i, (out_ref, rotary, scale) in enumerate(((q_ref, True, q_scale), (k_ref, True, None),
                                                   (v_ref, False, None))):
        for j0 in range(0, n_pair, IN_PROJ_CHUNK // LANES):
            pieces.append(attn_piece(i, j0, rotary, scale))
        pieces += [plane_piece(i, out_ref, j) for j in range(n_pair)]
    pieces += [hg_piece(c) for c in range(4 * HGRN_WIDTH // IN_PROJ_CHUNK)]
    if not do_proj:
        pieces = []
    if not do_hgrn:
        for run in pieces:
            run()
        return

    lbl = lbl_ref[...]
    mx = jnp.max(lbl, axis=0, keepdims=True)
    e = jnp.exp(lbl - mx)
    lb_all = e[0:1, :] / jnp.sum(e, axis=0, keepdims=True)

    row = lax.broadcasted_iota(jnp.int32, (t_blk, t_blk), 0)
    col = lax.broadcasted_iota(jnp.int32, (t_blk, t_blk), 1)
    tri = (col <= row).astype(BF16)
    pair_key = jnp.zeros((t_blk, t_blk), jnp.int32)
    half = HGRN_SUB
    while half < t_blk:
        pair_key = jnp.where(jnp.logical_and((row ^ col) >= half, col < row), half, pair_key)
        half *= 2

    hg_in = hg_scr.at[1 - write_slot]
    y_out = y_ref.at[0]
    n_blocks = hg_scr.shape[1] // t_blk
    stages = [_hgrn_blocks(i * t_blk, lb_all, nw_ref, hg_in, y_out, st_ref, b_scr, q_scr, k_scr, a_scr,
                           tri, pair_key) for i in range(n_blocks)]
    for gen in stages:
        for _ in gen:
            if pieces:
                pieces.pop(0)()
    assert not pieces


def _in_proj_hgrn(x, norm1_w, w_in, cos_t, sin_t, lb_logits, hgrn_norm_w):
    batch, seq, _ = x.shape
    tm = ROW_TILE
    n_pair = ATTN_WIDTH // LANES
    tiles_per_seq = seq // tm
    n_tiles = batch * tiles_per_seq

    def proj_tile(s):
        s = jnp.minimum(s, n_tiles - 1)
        return s // tiles_per_seq, s % tiles_per_seq

    def hgrn_tile(s):
        s = jnp.maximum(s - 1, 0)
        return s // tiles_per_seq, s % tiles_per_seq

    plane_spec = pl.BlockSpec((1, n_pair, PLANES, tm // PLANES, LANES),
                              lambda s: (proj_tile(s)[0], 0, 0, proj_tile(s)[1], 0))
    plane_shape = jax.ShapeDtypeStruct((batch, n_pair, PLANES, seq // PLANES, LANES), F32)
    tab_spec = pl.BlockSpec((tm, LANES), lambda s: (proj_tile(s)[1], 0))
    blk_scratch = pltpu.VMEM((HGRN_HEADS, HGRN_BLOCK, HGRN_DK), F32)
    return pl.pallas_call(
        functools.partial(_in_proj_hgrn_kernel, tiles_per_seq=tiles_per_seq),
        grid=(n_tiles + 1,),
        in_specs=[pl.BlockSpec((1, tm, D_MODEL), lambda s: (*proj_tile(s), 0)), _resident((1, D_MODEL)),
                  _resident(w_in.shape), tab_spec, tab_spec,
                  _resident(lb_logits.shape), _resident((1, HGRN_WIDTH))],
        out_specs=[plane_spec] * 3 + [pl.BlockSpec((1, tm, HGRN_WIDTH), lambda s: (*hgrn_tile(s), 0))],
        out_shape=[plane_shape] * 3 + [jax.ShapeDtypeStruct((batch, seq, HGRN_WIDTH), BF16)],
        scratch_shapes=[pltpu.VMEM((2, tm, 4 * HGRN_WIDTH), F32),
                        pltpu.VMEM((D_MODEL, IN_PROJ_WIDTH), BF16),
                        pltpu.VMEM((3 * n_pair, tm, LANES), F32),
                        pltpu.VMEM((HGRN_HEADS, HGRN_DK, HGRN_DK), F32),
                        blk_scratch, blk_scratch, blk_scratch,
                        pltpu.VMEM((HGRN_HEADS, HGRN_BLOCK, HGRN_BLOCK), F32)],
        compiler_params=pltpu.CompilerParams(
            dimension_semantics=("arbitrary",), vmem_limit_bytes=VMEM_LIMIT),
        name="in_proj_hgrn2",
    )(x, norm1_w, w_in, cos_t, sin_t, lb_logits, hgrn_norm_w)


def _out_ffn_kernel(ya_ref, yb_ref, x_ref, wo_ref, n2_ref, wgu_ref, wd_ref, nf_ref, out_ref):
    chunk = FFN_CHUNK
    mixed = jnp.concatenate([ya_ref[0, j].astype(BF16) for j in range(ATTN_WIDTH // LANES)]
                            + [yb_ref[0]], axis=1)
    h = x_ref[0] + _dot(mixed, wo_ref[0].astype(BF16))
    u = (_rms_scale(h) * n2_ref[...]).astype(BF16)
    ffn = jnp.zeros_like(h)
    for c in range(FFN_HIDDEN // chunk):
        cols = slice(c * chunk, (c + 1) * chunk)
        up_cols = slice(FFN_HIDDEN + c * chunk, FFN_HIDDEN + (c + 1) * chunk)
        gate = _dot(u, wgu_ref[0, :, cols].astype(BF16))
        up = _dot(u, wgu_ref[0, :, up_cols].astype(BF16))
        act = (gate * _sigmoid(gate) * up).astype(BF16)
        ffn = ffn + _dot(act, wd_ref[0, cols, :].astype(BF16))
    out_ref[0] = _rms_scale(h + ffn) * nf_ref[...]


def _out_ffn(ya, yb, x, w_out, norm2_w, w_gate_up, w_down, final_w):
    batch, seq, _ = x.shape
    tm = ROW_TILE
    n_pair = ATTN_WIDTH // LANES
    row_spec = lambda width: pl.BlockSpec((1, tm, width), lambda b, i: (b, i, 0))
    return pl.pallas_call(
        _out_ffn_kernel,
        grid=(batch, seq // tm),
        in_specs=[pl.BlockSpec((1, n_pair, tm, LANES), lambda b, i: (b, 0, i, 0)),
                  row_spec(HGRN_WIDTH), row_spec(D_MODEL),
                  _resident(w_out.shape), _resident((1, D_MODEL)), _resident(w_gate_up.shape),
                  _resident(w_down.shape), _resident((1, D_MODEL))],
        out_specs=row_spec(D_MODEL),
        out_shape=jax.ShapeDtypeStruct((batch, seq, D_MODEL), F32),
        compiler_params=pltpu.CompilerParams(
            dimension_semantics=("arbitrary", "arbitrary"), vmem_limit_bytes=VMEM_LIMIT),
        name="out_proj_ffn",
    )(ya, yb, x, w_out, norm2_w, w_gate_up, w_down, final_w)


def kernel(x, norm1_w, w_in, lb_logits, hgrn_norm_w, w_out, norm2_w, w_gate_up, w_down, final_norm_w):
    batch, seq, d_model = x.shape
    assert d_model == D_MODEL and norm1_w.shape[0] == 1
    assert all(w // d == ATTN_BLOCK for w, d in DILATED_PAIRS)
    assert tuple(d for _, d in DILATED_PAIRS) == (1, 4, 16) and seq % (PLANES * ATTN_BLOCK) == 0
    cos_t, sin_t = _rope_tables(seq)
    q, k, v, yb = _in_proj_hgrn(x, norm1_w, w_in, cos_t, sin_t, lb_logits, hgrn_norm_w)
    ya = _dilated_attention(q, k, v)
    return _out_ffn(ya, yb, x, w_out, norm2_w, w_gate_up, w_down, final_norm_w.reshape(1, d_model))
```

```python
import functools

import numpy as np
import jax
import jax.numpy as jnp
from jax import lax
from jax.experimental import pallas as pl
from jax.experimental.pallas import tpu as pltpu

D_MODEL = 1024
ATTN_WIDTH = 512
ATTN_HEADS = 8
HEAD_DIM = 64
DILATED_PAIRS = ((128, 1), (512, 4), (2048, 16))
ATTN_BLOCK = 128
ROPE_THETA = 10000.0
HGRN_WIDTH = 512
HGRN_DK = 128
HGRN_HEADS = 4
FFN_HIDDEN = 2816
IN_PROJ_WIDTH = 3 * ATTN_WIDTH + 4 * HGRN_WIDTH
NORM_EPS = 1e-6

LANES = 128
PLANES = 16
ATTN_SKEW_SCORES = 2
ATTN_SKEW_FINISH = 2
ROW_TILE = 512
IN_PROJ_CHUNK = 512
FFN_CHUNK = 256
HGRN_BLOCK = 128
HGRN_SUB = 8
VMEM_LIMIT = 56 * 1024 * 1024

F32 = jnp.float32
BF16 = jnp.bfloat16


def _dot(a, b):
    return jnp.dot(a, b, preferred_element_type=F32)


def _dot_nt(a, b):
    return lax.dot_general(a, b, (((1,), (1,)), ((), ())), preferred_element_type=F32)


def _dot_tn(a, b):
    return lax.dot_general(a, b, (((0,), (0,)), ((), ())), preferred_element_type=F32)


def _rms_scale(x):
    return x * lax.rsqrt(jnp.mean(x * x, axis=-1, keepdims=True) + NORM_EPS)


def _sigmoid(x):
    return 1.0 / (1.0 + jnp.exp(-x))


def _resident(shape):
    return pl.BlockSpec(shape, lambda *_: (0,) * len(shape), pipeline_mode=pl.Buffered(1))


def _rope_tables(seq):
    half = HEAD_DIM // 2
    inv_freq = ROPE_THETA ** (-np.arange(half, dtype=np.float64) / half)
    ang = np.arange(seq, dtype=np.float64)[:, None] * inv_freq[None, :]
    cos, sin = np.cos(ang), np.sin(ang)
    cos_t = np.concatenate([cos, cos, cos, cos], axis=1)
    sin_t = np.concatenate([-sin, sin, -sin, sin], axis=1)
    return jnp.asarray(cos_t, F32), jnp.asarray(sin_t, F32)


def _rotate_half_pairs(t, cos, sin):
    lane = lax.broadcasted_iota(jnp.int32, t.shape, 1)
    fwd = pltpu.roll(t, LANES - HEAD_DIM // 2, 1)
    bwd = pltpu.roll(t, HEAD_DIM // 2, 1)
    partner = jnp.where((lane & (HEAD_DIM // 2)) == 0, fwd, bwd)
    return t * cos + partner * sin


def _block_scores(q, keys, head_a):
    qb = q.astype(BF16)
    zero = jnp.zeros_like(qb)
    lhs = jnp.concatenate([jnp.where(head_a, qb, zero), jnp.where(head_a, zero, qb)], axis=0)
    return _dot_nt(lhs, keys)


def _block_softmax_pv(s, bias, vals_a, vals_b):
    blk = ATTN_BLOCK
    accs, maxes = [], []
    for half, vals in enumerate((vals_a, vals_b)):
        sh = s[half * blk:(half + 1) * blk] + bias
        m = jnp.max(sh, axis=-1, keepdims=True)
        accs.append(_dot(jnp.exp2(sh - m).astype(BF16), vals))
        maxes.append(m)
    return accs, maxes


def _block_finish(accs, maxes, head_a):
    out = jnp.where(head_a, accs[0], accs[1])
    den = pltpu.roll(jnp.where(head_a, accs[1], accs[0]), HEAD_DIM, 1)
    mx = jnp.where(head_a, maxes[0], maxes[1])
    return out, den, mx


def _block_row_index(dilation):
    per_plane = ATTN_BLOCK * dilation // PLANES
    shift = per_plane.bit_length() - 1
    return lambda i: ((i & (per_plane - 1)) * (PLANES // dilation)) + (i >> shift)


def _attn_kernel(q_ref, kp_ref, kc_ref, vp_ref, vc_ref, y_ref, o_scr, d_scr, m_scr, bias_scr):
    blk = ATTN_BLOCK
    dilations = (16, 4, 1)
    head_a = lax.broadcasted_iota(jnp.int32, (blk, LANES), 1) < HEAD_DIM

    @pl.when((pl.program_id(0) == 0) & (pl.program_id(1) == 0) & (pl.program_id(2) == 0))
    def _():
        row = lax.broadcasted_iota(jnp.int32, (blk, 2 * blk), 0)
        col = lax.broadcasted_iota(jnp.int32, (blk, 2 * blk), 1)
        in_prev = col < blk
        col = col & (blk - 1)
        for pat, dilation in enumerate(dilations):
            index_of_row = _block_row_index(dilation)
            delta = index_of_row(row) - index_of_row(col)
            for variant, prev_offset in enumerate((4 * blk, blk)):
                d = jnp.where(in_prev, delta + prev_offset, delta)
                bias_scr[variant, pat] = jnp.where(jnp.logical_and(d >= 0, d <= blk), 0.0, -jnp.inf)

    has_prev = (pl.program_id(2) > 0).astype(jnp.int32)

    def gather(ref, planes, lo, size):
        parts = [ref[0, 0, r, lo:lo + size, :] for r in planes]
        return parts[0] if len(parts) == 1 else jnp.concatenate(parts, axis=0)

    def keys_values(k_ref, v_ref, planes, lo, size):
        k = gather(k_ref, planes, lo, size).astype(BF16)
        v = gather(v_ref, planes, lo, size).astype(BF16)
        one = jnp.ones_like(v)
        return k, jnp.where(head_a, v, one), jnp.where(head_a, one, v)

    blocks = []
    for pat, dilation in enumerate(dilations):
        per_plane = blk * dilation // PLANES
        for res in range(dilation):
            planes = list(range(res, PLANES, dilation))
            blocks += [(pat, planes, per_plane, n) for n in range(blk // per_plane)]

    carried = {}

    def scores_stage(pat, planes, per_plane, n):
        lo = n * per_plane
        prev = keys_values(kp_ref, vp_ref, planes, blk - per_plane, per_plane) if n == 0 else carried["kv"]
        cur = carried["kv"] = keys_values(kc_ref, vc_ref, planes, lo, per_plane)
        keys, vals_a, vals_b = (jnp.concatenate([p, c], axis=0) for p, c in zip(prev, cur))
        return _block_scores(gather(q_ref, planes, lo, per_plane), keys, head_a), vals_a, vals_b

    def softmax_stage(pat, planes, per_plane, n, s, vals_a, vals_b):
        bias = bias_scr[has_prev, pat] if n == 0 else bias_scr[1, pat]
        return _block_softmax_pv(s, bias, vals_a, vals_b)

    def finish_stage(pat, planes, per_plane, n, accs, maxes):
        lo = n * per_plane
        out, den, mx = _block_finish(accs, maxes, head_a)
        for j, r in enumerate(planes):
            rows = slice(j * per_plane, (j + 1) * per_plane)
            o_scr[pat, r, lo:lo + per_plane, :] = out[rows]
            d_scr[pat, r, lo:lo + per_plane, :] = den[rows]
            m_scr[pat, r, lo:lo + per_plane, :] = mx[rows]

    in_scores, in_softmax = {}, {}
    for step in range(len(blocks) + ATTN_SKEW_SCORES + ATTN_SKEW_FINISH):
        if step < len(blocks):
            in_scores[step] = scores_stage(*blocks[step])
        mid = step - ATTN_SKEW_SCORES
        if 0 <= mid < len(blocks):
            in_softmax[mid] = softmax_stage(*blocks[mid], *in_scores.pop(mid))
        last = mid - ATTN_SKEW_FINISH
        if 0 <= last < len(blocks):
            finish_stage(*blocks[last], *in_softmax.pop(last))

    for r in range(PLANES):
        m = jnp.maximum(jnp.maximum(m_scr[0, r], m_scr[1, r]), m_scr[2, r])
        num = den = None
        for pat in range(len(dilations)):
            w = jnp.exp2(m_scr[pat, r] - m)
            num = w * o_scr[pat, r] if num is None else num + w * o_scr[pat, r]
            den = w * d_scr[pat, r] if den is None else den + w * d_scr[pat, r]
        y_ref[0, 0, pl.ds(r, blk, stride=PLANES), :] = num / den


def _dilated_attention(q, k, v):
    batch, n_pair, planes, rows, lanes = q.shape
    blk = ATTN_BLOCK
    tile = planes * blk
    cur = pl.BlockSpec((1, 1, planes, blk, lanes), lambda b, h, t: (b, h, 0, t, 0))
    prev = pl.BlockSpec((1, 1, planes, blk, lanes), lambda b, h, t: (b, h, 0, jnp.maximum(t - 1, 0), 0))
    return pl.pallas_call(
        _attn_kernel,
        grid=(batch, n_pair, rows // blk),
        in_specs=[cur, prev, cur, prev, cur],
        out_specs=pl.BlockSpec((1, 1, tile, lanes), lambda b, h, t: (b, h, t, 0)),
        out_shape=jax.ShapeDtypeStruct((batch, n_pair, planes * rows, lanes), F32),
        scratch_shapes=[pltpu.VMEM((len(DILATED_PAIRS), planes, blk, lanes), F32)] * 3
        + [pltpu.VMEM((2, len(DILATED_PAIRS), blk, 2 * blk), F32)],
        compiler_params=pltpu.CompilerParams(
            dimension_semantics=("arbitrary",) * 3, vmem_limit_bytes=VMEM_LIMIT),
        name="dilated_attn",
    )(q, k, k, v, v)


def _hgrn_blocks(base, lb_all, nw_ref, hg_ref, y_ref, st_ref, b_scr, q_scr, k_scr, a_scr, tri, pair_key):
    t_blk = HGRN_BLOCK
    dk = HGRN_DK
    sub = HGRN_SUB
    w = HGRN_WIDTH
    rows = pl.ds(base, t_blk)
    heads = range(HGRN_HEADS)
    cols = [slice(h * dk, (h + 1) * dk) for h in heads]
    qf, key, v_bf, b, o = {}, {}, {}, {}, {}

    for h in heads:
        lb = lb_all[:, cols[h]]
        qr = hg_ref[rows, h * dk:(h + 1) * dk]
        fl = hg_ref[rows, w + h * dk:w + (h + 1) * dk]
        f = lb + (1.0 - lb) * _sigmoid(fl)
        lf = jnp.log2(f)
        key[h] = 1.0 - f
        qf[h] = qr * _sigmoid(qr)
        v_bf[h] = hg_ref[rows, 2 * w + h * dk:2 * w + (h + 1) * dk].astype(BF16)
        hi = lf.astype(BF16)
        r1 = lf - hi.astype(F32)
        mid = r1.astype(BF16)
        lo = (r1 - mid.astype(F32)).astype(BF16)
        b[h] = _dot(tri, hi) + _dot(tri, mid) + _dot(tri, lo)
        b_scr[h] = b[h]
        q_scr[h] = qf[h]
        k_scr[h] = jnp.log2(key[h]) - b[h]
        yield

    for h in heads:
        b_last = b_scr[h, t_blk - 1:t_blk, :]
        st = st_ref[h]
        q_inter = (qf[h] * jnp.exp2(b[h])).astype(BF16)
        o[h] = _dot_nt(q_inter, st.astype(BF16))
        k_upd = (key[h] * jnp.exp2(b_last - b[h])).astype(BF16)
        st_ref[h] = st * jnp.exp2(b_last) + _dot_tn(v_bf[h], k_upd)
    yield

    sub_id = lax.broadcasted_iota(jnp.int32, (sub, t_blk), 0)
    lane_id = lax.broadcasted_iota(jnp.int32, (sub, t_blk), 1)
    keep = [jnp.logical_and((lane_id & (sub - 1)) == s, sub_id >= s) for s in range(sub)]
    for c in range(t_blk // sub):
        lo_row = c * sub
        in_chunk = (lane_id & -sub) == lo_row
        for h in heads:
            b_t = b_scr[h, lo_row:lo_row + sub, :]
            q_t = q_scr[h, lo_row:lo_row + sub, :]
            blk_rows = jnp.zeros((sub, t_blk), F32)
            for s in range(sub):
                c_s = k_scr[h, lo_row + s:lo_row + s + 1, :]
                col = jnp.sum(q_t * jnp.exp2(b_t + c_s), axis=-1, keepdims=True)
                blk_rows = jnp.where(keep[s], col, blk_rows)
            a_scr[h, lo_row:lo_row + sub, :] = jnp.where(in_chunk, blk_rows, 0.0)
        if c % 2 == 1:
            yield

    a_mat = {h: a_scr[h] for h in heads}
    rid = lax.broadcasted_iota(jnp.int32, (t_blk, dk), 0)
    half = sub
    while half < t_blk:
        span = 2 * half
        upper = (rid & half) != 0
        for h in heads:
            refs = [jnp.broadcast_to(b_scr[h, p * span + half - 1:p * span + half, :], (span, dk))
                    for p in range(t_blk // span)]
            d = b[h] - (refs[0] if len(refs) == 1 else jnp.concatenate(refs, axis=0))
            x = (jnp.where(upper, qf[h], key[h]) * jnp.exp2(jnp.where(upper, d, -d))).astype(BF16)
            a_mat[h] = jnp.where(pair_key == half, _dot_nt(x, x), a_mat[h])
        half = span
        yield

    for h in heads:
        gr = hg_ref[rows, 3 * w + h * dk:3 * w + (h + 1) * dk]
        out = o[h] + _dot(a_mat[h].astype(BF16), v_bf[h])
        out = _rms_scale(out) * nw_ref[:, cols[h]]
        y_ref[rows, cols[h]] = (out * (gr * _sigmoid(gr))).astype(BF16)


def _in_proj_hgrn_kernel(x_ref, n1_ref, w_ref, *refs, tiles_per_seq):
    step = pl.program_id(0)
    last = pl.num_programs(0) - 1
    w_scr, st_ref = refs[9], refs[11]

    @pl.when(step == 0)
    def _():
        for col in range(0, IN_PROJ_WIDTH, IN_PROJ_CHUNK):
            w_scr[:, col:col + IN_PROJ_CHUNK] = w_ref[0, :, col:col + IN_PROJ_CHUNK].astype(BF16)
        _in_proj_hgrn_step(True, False, x_ref, n1_ref, *refs)

    @pl.when(step % tiles_per_seq == 1)
    def _():
        st_ref[...] = jnp.zeros_like(st_ref)

    @pl.when((step > 0) & (step < last))
    def _():
        _in_proj_hgrn_step(True, True, x_ref, n1_ref, *refs)

    @pl.when(step == last)
    def _():
        _in_proj_hgrn_step(False, True, x_ref, n1_ref, *refs)


def _in_proj_hgrn_step(do_proj, do_hgrn, x_ref, n1_ref, cos_ref, sin_ref, lbl_ref, nw_ref,
                       q_ref, k_ref, v_ref, y_ref,
                       hg_scr, w_scr, stage_ref, st_ref, b_scr, q_scr, k_scr, a_scr):
    t_blk = HGRN_BLOCK
    step = pl.program_id(0)
    write_slot = step & 1

    u = (_rms_scale(x_ref[0]) * n1_ref[...]).astype(BF16)
    cos = cos_ref[...]
    sin = sin_ref[...]
    a = ATTN_WIDTH
    n_pair = a // LANES
    hg_out = hg_scr.at[write_slot]
    q_scale = HEAD_DIM ** -0.5 * float(np.log2(np.e))
    rows_per_plane = stage_ref.shape[1] // PLANES
    pieces = []

    def project(col, width):
        return _dot(u, w_scr[:, col:col + width])

    def attn_piece(i, j0, rotary, scale):
        def run():
            p = project(i * a + j0 * LANES, IN_PROJ_CHUNK)
            for jj in range(IN_PROJ_CHUNK // LANES):
                t = p[:, jj * LANES:(jj + 1) * LANES]
                if rotary:
                    t = _rotate_half_pairs(t, cos, sin)
                stage_ref[i * n_pair + j0 + jj] = t if scale is None else t * scale
        return run

    def plane_piece(i, out_ref, j):
        def run():
            for r in range(PLANES):
                out_ref[0, j, r] = stage_ref[i * n_pair + j, pl.ds(r, rows_per_plane, stride=PLANES), :]
        return run

    def hg_piece(c):
        def run():
            hg_out[:, c * IN_PROJ_CHUNK:(c + 1) * IN_PROJ_CHUNK] = project(3 * a + c * IN_PROJ_CHUNK,
                                                                           IN_PROJ_CHUNK)
        return run

    hg_pieces = [hg_piece(c) for c in range(4 * HGRN_WIDTH // IN_PROJ_CHUNK)]
    for i, (out_ref, rotary, scale) in enumerate(((q_ref, True, q_scale), (k_ref, True, None),
                                                   (v_ref, False, None))):
        for j0 in range(0, n_pair, IN_PROJ_CHUNK // LANES):
            pieces.append(attn_piece(i, j0, rotary, scale))
            pieces.append(hg_pieces.pop(0))
        pieces += [plane_piece(i, out_ref, j) for j in range(n_pair)]
    pieces += hg_pieces
    if not do_proj:
        pieces = []
    if not do_hgrn:
        for run in pieces:
            run()
        return

    lbl = lbl_ref[...]
    mx = jnp.max(lbl, axis=0, keepdims=True)
    e = jnp.exp(lbl - mx)
    lb_all = e[0:1, :] / jnp.sum(e, axis=0, keepdims=True)

    row = lax.broadcasted_iota(jnp.int32, (t_blk, t_blk), 0)
    col = lax.broadcasted_iota(jnp.int32, (t_blk, t_blk), 1)
    tri = (col <= row).astype(BF16)
    pair_key = jnp.zeros((t_blk, t_blk), jnp.int32)
    half = HGRN_SUB
    while half < t_blk:
        pair_key = jnp.where(jnp.logical_and((row ^ col) >= half, col < row), half, pair_key)
        half *= 2

    hg_in = hg_scr.at[1 - write_slot]
    y_out = y_ref.at[0]
    n_blocks = hg_scr.shape[1] // t_blk
    stages = [_hgrn_blocks(i * t_blk, lb_all, nw_ref, hg_in, y_out, st_ref, b_scr, q_scr, k_scr, a_scr,
                           tri, pair_key) for i in range(n_blocks)]
    for gen in stages:
        for _ in gen:
            if pieces:
                pieces.pop(0)()
    assert not pieces


def _in_proj_hgrn(x, norm1_w, w_in, cos_t, sin_t, lb_logits, hgrn_norm_w):
    batch, seq, _ = x.shape
    tm = ROW_TILE
    n_pair = ATTN_WIDTH // LANES
    tiles_per_seq = seq // tm
    n_tiles = batch * tiles_per_seq

    def proj_tile(s):
        s = jnp.minimum(s, n_tiles - 1)
        return s // tiles_per_seq, s % tiles_per_seq

    def hgrn_tile(s):
        s = jnp.maximum(s - 1, 0)
        return s // tiles_per_seq, s % tiles_per_seq

    plane_spec = pl.BlockSpec((1, n_pair, PLANES, tm // PLANES, LANES),
                              lambda s: (proj_tile(s)[0], 0, 0, proj_tile(s)[1], 0))
    plane_shape = jax.ShapeDtypeStruct((batch, n_pair, PLANES, seq // PLANES, LANES), F32)
    tab_spec = pl.BlockSpec((tm, LANES), lambda s: (proj_tile(s)[1], 0))
    blk_scratch = pltpu.VMEM((HGRN_HEADS, HGRN_BLOCK, HGRN_DK), F32)
    return pl.pallas_call(
        functools.partial(_in_proj_hgrn_kernel, tiles_per_seq=tiles_per_seq),
        grid=(n_tiles + 1,),
        in_specs=[pl.BlockSpec((1, tm, D_MODEL), lambda s: (*proj_tile(s), 0)), _resident((1, D_MODEL)),
                  _resident(w_in.shape), tab_spec, tab_spec,
                  _resident(lb_logits.shape), _resident((1, HGRN_WIDTH))],
        out_specs=[plane_spec] * 3 + [pl.BlockSpec((1, tm, HGRN_WIDTH), lambda s: (*hgrn_tile(s), 0))],
        out_shape=[plane_shape] * 3 + [jax.ShapeDtypeStruct((batch, seq, HGRN_WIDTH), BF16)],
        scratch_shapes=[pltpu.VMEM((2, tm, 4 * HGRN_WIDTH), F32),
                        pltpu.VMEM((D_MODEL, IN_PROJ_WIDTH), BF16),
                        pltpu.VMEM((3 * n_pair, tm, LANES), F32),
                        pltpu.VMEM((HGRN_HEADS, HGRN_DK, HGRN_DK), F32),
                        blk_scratch, blk_scratch, blk_scratch,
                        pltpu.VMEM((HGRN_HEADS, HGRN_BLOCK, HGRN_BLOCK), F32)],
        compiler_params=pltpu.CompilerParams(
            dimension_semantics=("arbitrary",), vmem_limit_bytes=VMEM_LIMIT),
        name="in_proj_hgrn2",
    )(x, norm1_w, w_in, cos_t, sin_t, lb_logits, hgrn_norm_w)


def _out_ffn_kernel(ya_ref, yb_ref, x_ref, wo_ref, n2_ref, wgu_ref, wd_ref, nf_ref, out_ref):
    chunk = FFN_CHUNK
    mixed = jnp.concatenate([ya_ref[0, j].astype(BF16) for j in range(ATTN_WIDTH // LANES)]
                            + [yb_ref[0]], axis=1)
    h = x_ref[0] + _dot(mixed, wo_ref[0].astype(BF16))
    u = (_rms_scale(h) * n2_ref[...]).astype(BF16)
    ffn = jnp.zeros_like(h)
    for c in range(FFN_HIDDEN // chunk):
        cols = slice(c * chunk, (c + 1) * chunk)
        up_cols = slice(FFN_HIDDEN + c * chunk, FFN_HIDDEN + (c + 1) * chunk)
        gate = _dot(u, wgu_ref[0, :, cols].astype(BF16))
        up = _dot(u, wgu_ref[0, :, up_cols].astype(BF16))
        act = (gate * _sigmoid(gate) * up).astype(BF16)
        ffn = ffn + _dot(act, wd_ref[0, cols, :].astype(BF16))
    out_ref[0] = _rms_scale(h + ffn) * nf_ref[...]


def _out_ffn(ya, yb, x, w_out, norm2_w, w_gate_up, w_down, final_w):
    batch, seq, _ = x.shape
    tm = ROW_TILE
    n_pair = ATTN_WIDTH // LANES
    row_spec = lambda width: pl.BlockSpec((1, tm, width), lambda b, i: (b, i, 0))
    return pl.pallas_call(
        _out_ffn_kernel,
        grid=(batch, seq // tm),
        in_specs=[pl.BlockSpec((1, n_pair, tm, LANES), lambda b, i: (b, 0, i, 0)),
                  row_spec(HGRN_WIDTH), row_spec(D_MODEL),
                  _resident(w_out.shape), _resident((1, D_MODEL)), _resident(w_gate_up.shape),
                  _resident(w_down.shape), _resident((1, D_MODEL))],
        out_specs=row_spec(D_MODEL),
        out_shape=jax.ShapeDtypeStruct((batch, seq, D_MODEL), F32),
        compiler_params=pltpu.CompilerParams(
            dimension_semantics=("arbitrary", "arbitrary"), vmem_limit_bytes=VMEM_LIMIT),
        name="out_proj_ffn",
    )(ya, yb, x, w_out, norm2_w, w_gate_up, w_down, final_w)


def kernel(x, norm1_w, w_in, lb_logits, hgrn_norm_w, w_out, norm2_w, w_gate_up, w_down, final_norm_w):
    batch, seq, d_model = x.shape
    assert d_model == D_MODEL and norm1_w.shape[0] == 1
    assert all(w // d == ATTN_BLOCK for w, d in DILATED_PAIRS)
    assert tuple(d for _, d in DILATED_PAIRS) == (1, 4, 16) and seq % (PLANES * ATTN_BLOCK) == 0
    cos_t, sin_t = _rope_tables(seq)
    q, k, v, yb = _in_proj_hgrn(x, norm1_w, w_in, cos_t, sin_t, lb_logits, hgrn_norm_w)
    ya = _dilated_attention(q, k, v)
    return _out_ffn(ya, yb, x, w_out, norm2_w, w_gate_up, w_down, final_norm_w.reshape(1, d_model))
```

```python
import functools

import numpy as np
import jax
import jax.numpy as jnp
from jax import lax
from jax.experimental import pallas as pl
from jax.experimental.pallas import tpu as pltpu

D_MODEL = 1024
ATTN_WIDTH = 512
ATTN_HEADS = 8
HEAD_DIM = 64
DILATED_PAIRS = ((128, 1), (512, 4), (2048, 16))
ATTN_BLOCK = 128
ROPE_THETA = 10000.0
HGRN_WIDTH = 512
HGRN_DK = 128
HGRN_HEADS = 4
FFN_HIDDEN = 2816
IN_PROJ_WIDTH = 3 * ATTN_WIDTH + 4 * HGRN_WIDTH
NORM_EPS = 1e-6

LANES = 128
PLANES = 16
ATTN_SKEW_SCORES = 2
ATTN_SKEW_FINISH = 2
ROW_TILE = 512
IN_PROJ_CHUNK = 512
FFN_CHUNK = 256
HGRN_BLOCK = 128
HGRN_SUB = 8
VMEM_LIMIT = 56 * 1024 * 1024

F32 = jnp.float32
BF16 = jnp.bfloat16


def _dot(a, b):
    return jnp.dot(a, b, preferred_element_type=F32)


def _dot_nt(a, b):
    return lax.dot_general(a, b, (((1,), (1,)), ((), ())), preferred_element_type=F32)


def _dot_tn(a, b):
    return lax.dot_general(a, b, (((0,), (0,)), ((), ())), preferred_element_type=F32)


def _rms_scale(x):
    return x * lax.rsqrt(jnp.mean(x * x, axis=-1, keepdims=True) + NORM_EPS)


def _sigmoid(x):
    return 1.0 / (1.0 + jnp.exp(-x))


def _resident(shape):
    return pl.BlockSpec(shape, lambda *_: (0,) * len(shape), pipeline_mode=pl.Buffered(1))


def _rope_tables(seq):
    half = HEAD_DIM // 2
    inv_freq = ROPE_THETA ** (-np.arange(half, dtype=np.float64) / half)
    ang = np.arange(seq, dtype=np.float64)[:, None] * inv_freq[None, :]
    cos, sin = np.cos(ang), np.sin(ang)
    cos_t = np.concatenate([cos, cos, cos, cos], axis=1)
    sin_t = np.concatenate([-sin, sin, -sin, sin], axis=1)
    return jnp.asarray(cos_t, F32), jnp.asarray(sin_t, F32)


def _rotate_half_pairs(t, cos, sin):
    lane = lax.broadcasted_iota(jnp.int32, t.shape, 1)
    fwd = pltpu.roll(t, LANES - HEAD_DIM // 2, 1)
    bwd = pltpu.roll(t, HEAD_DIM // 2, 1)
    partner = jnp.where((lane & (HEAD_DIM // 2)) == 0, fwd, bwd)
    return t * cos + partner * sin


def _block_scores(q, keys, head_a):
    qb = q.astype(BF16)
    zero = jnp.zeros_like(qb)
    lhs = jnp.concatenate([jnp.where(head_a, qb, zero), jnp.where(head_a, zero, qb)], axis=0)
    return _dot_nt(lhs, keys)


def _block_softmax_pv(s, bias, vals_a, vals_b):
    blk = ATTN_BLOCK
    accs, maxes = [], []
    for half, vals in enumerate((vals_a, vals_b)):
        sh = s[half * blk:(half + 1) * blk] + bias
        m = jnp.max(sh, axis=-1, keepdims=True)
        accs.append(_dot(jnp.exp2(sh - m).astype(BF16), vals))
        maxes.append(m)
    return accs, maxes


def _block_finish(accs, maxes, head_a):
    out = jnp.where(head_a, accs[0], accs[1])
    den = pltpu.roll(jnp.where(head_a, accs[1], accs[0]), HEAD_DIM, 1)
    mx = jnp.where(head_a, maxes[0], maxes[1])
    return out, den, mx


def _block_row_index(dilation):
    per_plane = ATTN_BLOCK * dilation // PLANES
    shift = per_plane.bit_length() - 1
    return lambda i: ((i & (per_plane - 1)) * (PLANES // dilation)) + (i >> shift)


def _attn_kernel(q_ref, kp_ref, kc_ref, vp_ref, vc_ref, y_ref, o_scr, d_scr, m_scr, bias_scr):
    blk = ATTN_BLOCK
    dilations = (16, 4, 1)
    head_a = lax.broadcasted_iota(jnp.int32, (blk, LANES), 1) < HEAD_DIM

    @pl.when((pl.program_id(0) == 0) & (pl.program_id(1) == 0) & (pl.program_id(2) == 0))
    def _():
        row = lax.broadcasted_iota(jnp.int32, (blk, 2 * blk), 0)
        col = lax.broadcasted_iota(jnp.int32, (blk, 2 * blk), 1)
        in_prev = col < blk
        col = col & (blk - 1)
        for pat, dilation in enumerate(dilations):
            index_of_row = _block_row_index(dilation)
            delta = index_of_row(row) - index_of_row(col)
            for variant, prev_offset in enumerate((4 * blk, blk)):
                d = jnp.where(in_prev, delta + prev_offset, delta)
                bias_scr[variant, pat] = jnp.where(jnp.logical_and(d >= 0, d <= blk), 0.0, -jnp.inf)

    has_prev = (pl.program_id(2) > 0).astype(jnp.int32)

    def gather(ref, planes, lo, size):
        parts = [ref[0, 0, r, lo:lo + size, :] for r in planes]
        return parts[0] if len(parts) == 1 else jnp.concatenate(parts, axis=0)

    def keys_values(k_ref, v_ref, planes, lo, size):
        k = gather(k_ref, planes, lo, size).astype(BF16)
        v = gather(v_ref, planes, lo, size).astype(BF16)
        one = jnp.ones_like(v)
        return k, jnp.where(head_a, v, one), jnp.where(head_a, one, v)

    blocks = []
    for pat, dilation in enumerate(dilations):
        per_plane = blk * dilation // PLANES
        for res in range(dilation):
            planes = list(range(res, PLANES, dilation))
            blocks += [(pat, planes, per_plane, n) for n in range(blk // per_plane)]

    carried = {}

    def scores_stage(pat, planes, per_plane, n):
        lo = n * per_plane
        prev = keys_values(kp_ref, vp_ref, planes, blk - per_plane, per_plane) if n == 0 else carried["kv"]
        cur = carried["kv"] = keys_values(kc_ref, vc_ref, planes, lo, per_plane)
        keys, vals_a, vals_b = (jnp.concatenate([p, c], axis=0) for p, c in zip(prev, cur))
        return _block_scores(gather(q_ref, planes, lo, per_plane), keys, head_a), vals_a, vals_b

    def softmax_stage(pat, planes, per_plane, n, s, vals_a, vals_b):
        bias = bias_scr[has_prev, pat] if n == 0 else bias_scr[1, pat]
        return _block_softmax_pv(s, bias, vals_a, vals_b)

    def finish_stage(pat, planes, per_plane, n, accs, maxes):
        lo = n * per_plane
        out, den, mx = _block_finish(accs, maxes, head_a)
        for j, r in enumerate(planes):
            rows = slice(j * per_plane, (j + 1) * per_plane)
            o_scr[pat, r, lo:lo + per_plane, :] = out[rows]
            d_scr[pat, r, lo:lo + per_plane, :] = den[rows]
            m_scr[pat, r, lo:lo + per_plane, :] = mx[rows]

    in_scores, in_softmax = {}, {}
    for step in range(len(blocks) + ATTN_SKEW_SCORES + ATTN_SKEW_FINISH):
        if step < len(blocks):
            in_scores[step] = scores_stage(*blocks[step])
        mid = step - ATTN_SKEW_SCORES
        if 0 <= mid < len(blocks):
            in_softmax[mid] = softmax_stage(*blocks[mid], *in_scores.pop(mid))
        last = mid - ATTN_SKEW_FINISH
        if 0 <= last < len(blocks):
            finish_stage(*blocks[last], *in_softmax.pop(last))

    for r in range(PLANES):
        m = jnp.maximum(jnp.maximum(m_scr[0, r], m_scr[1, r]), m_scr[2, r])
        num = den = None
        for pat in range(len(dilations)):
            w = jnp.exp2(m_scr[pat, r] - m)
            num = w * o_scr[pat, r] if num is None else num + w * o_scr[pat, r]
            den = w * d_scr[pat, r] if den is None else den + w * d_scr[pat, r]
        y_ref[0, 0, pl.ds(r, blk, stride=PLANES), :] = num / den


def _dilated_attention(q, k, v):
    batch, n_pair, planes, rows, lanes = q.shape
    blk = ATTN_BLOCK
    tile = planes * blk
    cur = pl.BlockSpec((1, 1, planes, blk, lanes), lambda b, h, t: (b, h, 0, t, 0))
    prev = pl.BlockSpec((1, 1, planes, blk, lanes), lambda b, h, t: (b, h, 0, jnp.maximum(t - 1, 0), 0))
    return pl.pallas_call(
        _attn_kernel,
        grid=(batch, n_pair, rows // blk),
        in_specs=[cur, prev, cur, prev, cur],
        out_specs=pl.BlockSpec((1, 1, tile, lanes), lambda b, h, t: (b, h, t, 0)),
        out_shape=jax.ShapeDtypeStruct((batch, n_pair, planes * rows, lanes), F32),
        scratch_shapes=[pltpu.VMEM((len(DILATED_PAIRS), planes, blk, lanes), F32)] * 3
        + [pltpu.VMEM((2, len(DILATED_PAIRS), blk, 2 * blk), F32)],
        compiler_params=pltpu.CompilerParams(
            dimension_semantics=("arbitrary",) * 3, vmem_limit_bytes=VMEM_LIMIT),
        name="dilated_attn",
    )(q, k, k, v, v)


def _hgrn_blocks(base, lb_all, nw_ref, hg_ref, y_ref, st_ref, b_scr, q_scr, k_scr, a_scr, tri, pair_key):
    t_blk = HGRN_BLOCK
    dk = HGRN_DK
    sub = HGRN_SUB
    w = HGRN_WIDTH
    rows = pl.ds(base, t_blk)
    heads = range(HGRN_HEADS)
    cols = [slice(h * dk, (h + 1) * dk) for h in heads]
    qf, key, v_bf, b, o = {}, {}, {}, {}, {}

    for h in heads:
        lb = lb_all[:, cols[h]]
        qr = hg_ref[rows, h * dk:(h + 1) * dk]
        fl = hg_ref[rows, w + h * dk:w + (h + 1) * dk]
        f = lb + (1.0 - lb) * _sigmoid(fl)
        lf = jnp.log2(f)
        key[h] = 1.0 - f
        qf[h] = qr * _sigmoid(qr)
        v_bf[h] = hg_ref[rows, 2 * w + h * dk:2 * w + (h + 1) * dk].astype(BF16)
        hi = lf.astype(BF16)
        r1 = lf - hi.astype(F32)
        mid = r1.astype(BF16)
        lo = (r1 - mid.astype(F32)).astype(BF16)
        b[h] = _dot(tri, hi) + _dot(tri, mid) + _dot(tri, lo)
        b_scr[h] = b[h]
        q_scr[h] = qf[h]
        k_scr[h] = jnp.log2(key[h]) - b[h]
        yield

    for h in heads:
        b_last = b_scr[h, t_blk - 1:t_blk, :]
        st = st_ref[h]
        q_inter = (qf[h] * jnp.exp2(b[h])).astype(BF16)
        o[h] = _dot_nt(q_inter, st.astype(BF16))
        k_upd = (key[h] * jnp.exp2(b_last - b[h])).astype(BF16)
        st_ref[h] = st * jnp.exp2(b_last) + _dot_tn(v_bf[h], k_upd)
    yield

    sub_id = lax.broadcasted_iota(jnp.int32, (sub, t_blk), 0)
    lane_id = lax.broadcasted_iota(jnp.int32, (sub, t_blk), 1)
    keep = [jnp.logical_and((lane_id & (sub - 1)) == s, sub_id >= s) for s in range(sub)]
    for c in range(t_blk // sub):
        lo_row = c * sub
        in_chunk = (lane_id & -sub) == lo_row
        for h in heads:
            b_t = b_scr[h, lo_row:lo_row + sub, :]
            q_t = q_scr[h, lo_row:lo_row + sub, :]
            blk_rows = jnp.zeros((sub, t_blk), F32)
            for s in range(sub):
                c_s = k_scr[h, lo_row + s:lo_row + s + 1, :]
                col = jnp.sum(q_t * jnp.exp2(b_t + c_s), axis=-1, keepdims=True)
                blk_rows = jnp.where(keep[s], col, blk_rows)
            a_scr[h, lo_row:lo_row + sub, :] = jnp.where(in_chunk, blk_rows, 0.0)
        if c % 2 == 1:
            yield

    a_mat = {h: a_scr[h] for h in heads}
    rid = lax.broadcasted_iota(jnp.int32, (t_blk, dk), 0)
    half = sub
    while half < t_blk:
        span = 2 * half
        upper = (rid & half) != 0
        for h in heads:
            refs = [jnp.broadcast_to(b_scr[h, p * span + half - 1:p * span + half, :], (span, dk))
                    for p in range(t_blk // span)]
            d = b[h] - (refs[0] if len(refs) == 1 else jnp.concatenate(refs, axis=0))
            x = (jnp.where(upper, qf[h], key[h]) * jnp.exp2(jnp.where(upper, d, -d))).astype(BF16)
            a_mat[h] = jnp.where(pair_key == half, _dot_nt(x, x), a_mat[h])
        half = span
        yield

    for h in heads:
        gr = hg_ref[rows, 3 * w + h * dk:3 * w + (h + 1) * dk]
        out = o[h] + _dot(a_mat[h].astype(BF16), v_bf[h])
        out = _rms_scale(out) * nw_ref[:, cols[h]]
        y_ref[rows, cols[h]] = (out * (gr * _sigmoid(gr))).astype(BF16)


def _in_proj_hgrn_kernel(x_ref, n1_ref, w_ref, *refs, tiles_per_seq):
    step = pl.program_id(0)
    last = pl.num_programs(0) - 1
    w_scr, st_ref = refs[9], refs[11]

    @pl.when(step == 0)
    def _():
        for col in range(0, IN_PROJ_WIDTH, IN_PROJ_CHUNK):
            w_scr[:, col:col + IN_PROJ_CHUNK] = w_ref[0, :, col:col + IN_PROJ_CHUNK].astype(BF16)
        _in_proj_hgrn_step(True, False, x_ref, n1_ref, *refs)

    @pl.when(step % tiles_per_seq == 1)
    def _():
        st_ref[...] = jnp.zeros_like(st_ref)

    @pl.when((step > 0) & (step < last))
    def _():
        _in_proj_hgrn_step(True, True, x_ref, n1_ref, *refs)

    @pl.when(step == last)
    def _():
        _in_proj_hgrn_step(False, True, x_ref, n1_ref, *refs)


def _in_proj_hgrn_step(do_proj, do_hgrn, x_ref, n1_ref, cos_ref, sin_ref, lbl_ref, nw_ref,
                       q_ref, k_ref, v_ref, y_ref,
                       hg_scr, w_scr, stage_ref, st_ref, b_scr, q_scr, k_scr, a_scr, split_ref):
    t_blk = HGRN_BLOCK
    step = pl.program_id(0)
    write_slot = step & 1

    u = (_rms_scale(x_ref[0]) * n1_ref[...]).astype(BF16)
    cos = cos_ref[...]
    sin = sin_ref[...]
    a = ATTN_WIDTH
    n_pair = a // LANES
    hg_out = hg_scr.at[write_slot]
    q_scale = HEAD_DIM ** -0.5 * float(np.log2(np.e))
    rows_per_plane = stage_ref.shape[1] // PLANES
    pieces = []

    def project(col, width):
        return _dot(u, w_scr[:, col:col + width])

    def attn_piece(i, j0, rotary, scale):
        def run():
            p = project(i * a + j0 * LANES, IN_PROJ_CHUNK)
            for jj in range(IN_PROJ_CHUNK // LANES):
                t = p[:, jj * LANES:(jj + 1) * LANES]
                if rotary:
                    t = _rotate_half_pairs(t, cos, sin)
                stage_ref[i * n_pair + j0 + jj] = t if scale is None else t * scale
        return run

    def plane_piece(i, out_ref, j):
        slab = i * n_pair + j
        quarter = stage_ref.shape[1] // 4

        def run():
            for a in range(4):
                split_ref[slab, a] = stage_ref[slab, pl.ds(a, quarter, stride=4), :]
            for r in range(PLANES):
                out_ref[0, j, r] = split_ref[slab, r % 4, pl.ds(r // 4, rows_per_plane, stride=4), :]
        return run

    def hg_piece(c):
        def run():
            hg_out[:, c * IN_PROJ_CHUNK:(c + 1) * IN_PROJ_CHUNK] = project(3 * a + c * IN_PROJ_CHUNK,
                                                                           IN_PROJ_CHUNK)
        return run

    hg_pieces = [hg_piece(c) for c in range(4 * HGRN_WIDTH // IN_PROJ_CHUNK)]
    for i, (out_ref, rotary, scale) in enumerate(((q_ref, True, q_scale), (k_ref, True, None),
                                                   (v_ref, False, None))):
        for j0 in range(0, n_pair, IN_PROJ_CHUNK // LANES):
            pieces.append(attn_piece(i, j0, rotary, scale))
            pieces.append(hg_pieces.pop(0))
        pieces += [plane_piece(i, out_ref, j) for j in range(n_pair)]
    pieces += hg_pieces
    if not do_proj:
        pieces = []
    if not do_hgrn:
        for run in pieces:
            run()
        return

    lbl = lbl_ref[...]
    mx = jnp.max(lbl, axis=0, keepdims=True)
    e = jnp.exp(lbl - mx)
    lb_all = e[0:1, :] / jnp.sum(e, axis=0, keepdims=True)

    row = lax.broadcasted_iota(jnp.int32, (t_blk, t_blk), 0)
    col = lax.broadcasted_iota(jnp.int32, (t_blk, t_blk), 1)
    tri = (col <= row).astype(BF16)
    pair_key = jnp.zeros((t_blk, t_blk), jnp.int32)
    half = HGRN_SUB
    while half < t_blk:
        pair_key = jnp.where(jnp.logical_and((row ^ col) >= half, col < row), half, pair_key)
        half *= 2

    hg_in = hg_scr.at[1 - write_slot]
    y_out = y_ref.at[0]
    n_blocks = hg_scr.shape[1] // t_blk
    stages = [_hgrn_blocks(i * t_blk, lb_all, nw_ref, hg_in, y_out, st_ref, b_scr, q_scr, k_scr, a_scr,
                           tri, pair_key) for i in range(n_blocks)]
    for gen in stages:
        for _ in gen:
            if pieces:
                pieces.pop(0)()
    assert not pieces


def _in_proj_hgrn(x, norm1_w, w_in, cos_t, sin_t, lb_logits, hgrn_norm_w):
    batch, seq, _ = x.shape
    tm = ROW_TILE
    n_pair = ATTN_WIDTH // LANES
    tiles_per_seq = seq // tm
    n_tiles = batch * tiles_per_seq

    def proj_tile(s):
        s = jnp.minimum(s, n_tiles - 1)
        return s // tiles_per_seq, s % tiles_per_seq

    def hgrn_tile(s):
        s = jnp.maximum(s - 1, 0)
        return s // tiles_per_seq, s % tiles_per_seq

    plane_spec = pl.BlockSpec((1, n_pair, PLANES, tm // PLANES, LANES),
                              lambda s: (proj_tile(s)[0], 0, 0, proj_tile(s)[1], 0))
    plane_shape = jax.ShapeDtypeStruct((batch, n_pair, PLANES, seq // PLANES, LANES), F32)
    tab_spec = pl.BlockSpec((tm, LANES), lambda s: (proj_tile(s)[1], 0))
    blk_scratch = pltpu.VMEM((HGRN_HEADS, HGRN_BLOCK, HGRN_DK), F32)
    return pl.pallas_call(
        functools.partial(_in_proj_hgrn_kernel, tiles_per_seq=tiles_per_seq),
        grid=(n_tiles + 1,),
        in_specs=[pl.BlockSpec((1, tm, D_MODEL), lambda s: (*proj_tile(s), 0)), _resident((1, D_MODEL)),
                  _resident(w_in.shape), tab_spec, tab_spec,
                  _resident(lb_logits.shape), _resident((1, HGRN_WIDTH))],
        out_specs=[plane_spec] * 3 + [pl.BlockSpec((1, tm, HGRN_WIDTH), lambda s: (*hgrn_tile(s), 0))],
        out_shape=[plane_shape] * 3 + [jax.ShapeDtypeStruct((batch, seq, HGRN_WIDTH), BF16)],
        scratch_shapes=[pltpu.VMEM((2, tm, 4 * HGRN_WIDTH), F32),
                        pltpu.VMEM((D_MODEL, IN_PROJ_WIDTH), BF16),
                        pltpu.VMEM((3 * n_pair, tm, LANES), F32),
                        pltpu.VMEM((HGRN_HEADS, HGRN_DK, HGRN_DK), F32),
                        blk_scratch, blk_scratch, blk_scratch,
                        pltpu.VMEM((HGRN_HEADS, HGRN_BLOCK, HGRN_BLOCK), F32),
                        pltpu.VMEM((3 * n_pair, 4, tm // 4, LANES), F32)],
        compiler_params=pltpu.CompilerParams(
            dimension_semantics=("arbitrary",), vmem_limit_bytes=VMEM_LIMIT),
        name="in_proj_hgrn2",
    )(x, norm1_w, w_in, cos_t, sin_t, lb_logits, hgrn_norm_w)


def _out_ffn_kernel(ya_ref, yb_ref, x_ref, wo_ref, n2_ref, wgu_ref, wd_ref, nf_ref, out_ref):
    chunk = FFN_CHUNK
    mixed = jnp.concatenate([ya_ref[0, j].astype(BF16) for j in range(ATTN_WIDTH // LANES)]
                            + [yb_ref[0]], axis=1)
    h = x_ref[0] + _dot(mixed, wo_ref[0].astype(BF16))
    u = (_rms_scale(h) * n2_ref[...]).astype(BF16)
    ffn = jnp.zeros_like(h)
    for c in range(FFN_HIDDEN // chunk):
        cols = slice(c * chunk, (c + 1) * chunk)
        up_cols = slice(FFN_HIDDEN + c * chunk, FFN_HIDDEN + (c + 1) * chunk)
        gate = _dot(u, wgu_ref[0, :, cols].astype(BF16))
        up = _dot(u, wgu_ref[0, :, up_cols].astype(BF16))
        act = (gate * _sigmoid(gate) * up).astype(BF16)
        ffn = ffn + _dot(act, wd_ref[0, cols, :].astype(BF16))
    out_ref[0] = _rms_scale(h + ffn) * nf_ref[...]


def _out_ffn(ya, yb, x, w_out, norm2_w, w_gate_up, w_down, final_w):
    batch, seq, _ = x.shape
    tm = ROW_TILE
    n_pair = ATTN_WIDTH // LANES
    row_spec = lambda width: pl.BlockSpec((1, tm, width), lambda b, i: (b, i, 0))
    return pl.pallas_call(
        _out_ffn_kernel,
        grid=(batch, seq // tm),
        in_specs=[pl.BlockSpec((1, n_pair, tm, LANES), lambda b, i: (b, 0, i, 0)),
                  row_spec(HGRN_WIDTH), row_spec(D_MODEL),
                  _resident(w_out.shape), _resident((1, D_MODEL)), _resident(w_gate_up.shape),
                  _resident(w_down.shape), _resident((1, D_MODEL))],
        out_specs=row_spec(D_MODEL),
        out_shape=jax.ShapeDtypeStruct((batch, seq, D_MODEL), F32),
        compiler_params=pltpu.CompilerParams(
            dimension_semantics=("arbitrary", "arbitrary"), vmem_limit_bytes=VMEM_LIMIT),
        name="out_proj_ffn",
    )(ya, yb, x, w_out, norm2_w, w_gate_up, w_down, final_w)


def kernel(x, norm1_w, w_in, lb_logits, hgrn_norm_w, w_out, norm2_w, w_gate_up, w_down, final_norm_w):
    batch, seq, d_model = x.shape
    assert d_model == D_MODEL and norm1_w.shape[0] == 1
    assert all(w // d == ATTN_BLOCK for w, d in DILATED_PAIRS)
    assert tuple(d for _, d in DILATED_PAIRS) == (1, 4, 16) and seq % (PLANES * ATTN_BLOCK) == 0
    cos_t, sin_t = _rope_tables(seq)
    q, k, v, yb = _in_proj_hgrn(x, norm1_w, w_in, cos_t, sin_t, lb_logits, hgrn_norm_w)
    ya = _dilated_attention(q, k, v)
    return _out_ffn(ya, yb, x, w_out, norm2_w, w_gate_up, w_down, final_norm_w.reshape(1, d_model))
```

```python
import functools

import numpy as np
import jax
import jax.numpy as jnp
from jax import lax
from jax.experimental import pallas as pl
from jax.experimental.pallas import tpu as pltpu

D_MODEL = 1024
ATTN_WIDTH = 512
ATTN_HEADS = 8
HEAD_DIM = 64
DILATED_PAIRS = ((128, 1), (512, 4), (2048, 16))
ATTN_BLOCK = 128
ROPE_THETA = 10000.0
HGRN_WIDTH = 512
HGRN_DK = 128
HGRN_HEADS = 4
FFN_HIDDEN = 2816
IN_PROJ_WIDTH = 3 * ATTN_WIDTH + 4 * HGRN_WIDTH
NORM_EPS = 1e-6

LANES = 128
PLANES = 16
ATTN_SKEW_SCORES = 2
ATTN_SKEW_FINISH = 2
ROW_TILE = 512
IN_PROJ_CHUNK = 512
FFN_CHUNK = 256
HGRN_BLOCK = 128
HGRN_SUB = 8
VMEM_LIMIT = 56 * 1024 * 1024

F32 = jnp.float32
BF16 = jnp.bfloat16


def _dot(a, b):
    return jnp.dot(a, b, preferred_element_type=F32)


def _dot_nt(a, b):
    return lax.dot_general(a, b, (((1,), (1,)), ((), ())), preferred_element_type=F32)


def _dot_tn(a, b):
    return lax.dot_general(a, b, (((0,), (0,)), ((), ())), preferred_element_type=F32)


def _rms_scale(x):
    return x * lax.rsqrt(jnp.mean(x * x, axis=-1, keepdims=True) + NORM_EPS)


def _sigmoid(x):
    return 1.0 / (1.0 + jnp.exp(-x))


def _resident(shape):
    return pl.BlockSpec(shape, lambda *_: (0,) * len(shape), pipeline_mode=pl.Buffered(1))


def _rope_tables(seq):
    half = HEAD_DIM // 2
    inv_freq = ROPE_THETA ** (-np.arange(half, dtype=np.float64) / half)
    ang = np.arange(seq, dtype=np.float64)[:, None] * inv_freq[None, :]
    cos, sin = np.cos(ang), np.sin(ang)
    cos_t = np.concatenate([cos, cos, cos, cos], axis=1)
    sin_t = np.concatenate([-sin, sin, -sin, sin], axis=1)
    return jnp.asarray(cos_t, F32), jnp.asarray(sin_t, F32)


def _rotate_half_pairs(t, cos, sin):
    lane = lax.broadcasted_iota(jnp.int32, t.shape, 1)
    fwd = pltpu.roll(t, LANES - HEAD_DIM // 2, 1)
    bwd = pltpu.roll(t, HEAD_DIM // 2, 1)
    partner = jnp.where((lane & (HEAD_DIM // 2)) == 0, fwd, bwd)
    return t * cos + partner * sin


def _block_scores(q, keys, head_a):
    qb = q.astype(BF16)
    zero = jnp.zeros_like(qb)
    lhs = jnp.concatenate([jnp.where(head_a, qb, zero), jnp.where(head_a, zero, qb)], axis=0)
    return _dot_nt(lhs, keys)


def _block_softmax_pv(s, bias, vals_a, vals_b):
    blk = ATTN_BLOCK
    accs, maxes = [], []
    for half, vals in enumerate((vals_a, vals_b)):
        sh = s[half * blk:(half + 1) * blk] + bias
        m = jnp.max(sh, axis=-1, keepdims=True)
        accs.append(_dot(jnp.exp2(sh - m).astype(BF16), vals))
        maxes.append(m)
    return accs, maxes


def _block_finish(accs, maxes, head_a):
    out = jnp.where(head_a, accs[0], accs[1])
    den = pltpu.roll(jnp.where(head_a, accs[1], accs[0]), HEAD_DIM, 1)
    mx = jnp.where(head_a, maxes[0], maxes[1])
    return out, den, mx


def _block_row_index(dilation):
    per_plane = ATTN_BLOCK * dilation // PLANES
    shift = per_plane.bit_length() - 1
    return lambda i: ((i & (per_plane - 1)) * (PLANES // dilation)) + (i >> shift)


def _attn_kernel(q_ref, kp_ref, kc_ref, vp_ref, vc_ref, y_ref, o_scr, d_scr, m_scr, bias_scr, y_scr):
    blk = ATTN_BLOCK
    dilations = (16, 4, 1)
    head_a = lax.broadcasted_iota(jnp.int32, (blk, LANES), 1) < HEAD_DIM

    @pl.when((pl.program_id(0) == 0) & (pl.program_id(1) == 0) & (pl.program_id(2) == 0))
    def _():
        row = lax.broadcasted_iota(jnp.int32, (blk, 2 * blk), 0)
        col = lax.broadcasted_iota(jnp.int32, (blk, 2 * blk), 1)
        in_prev = col < blk
        col = col & (blk - 1)
        for pat, dilation in enumerate(dilations):
            index_of_row = _block_row_index(dilation)
            delta = index_of_row(row) - index_of_row(col)
            for variant, prev_offset in enumerate((4 * blk, blk)):
                d = jnp.where(in_prev, delta + prev_offset, delta)
                bias_scr[variant, pat] = jnp.where(jnp.logical_and(d >= 0, d <= blk), 0.0, -jnp.inf)

    has_prev = (pl.program_id(2) > 0).astype(jnp.int32)

    def gather(ref, planes, lo, size):
        parts = [ref[0, 0, r, lo:lo + size, :] for r in planes]
        return parts[0] if len(parts) == 1 else jnp.concatenate(parts, axis=0)

    def keys_values(k_ref, v_ref, planes, lo, size):
        k = gather(k_ref, planes, lo, size).astype(BF16)
        v = gather(v_ref, planes, lo, size).astype(BF16)
        one = jnp.ones_like(v)
        return k, jnp.where(head_a, v, one), jnp.where(head_a, one, v)

    blocks = []
    for pat, dilation in enumerate(dilations):
        per_plane = blk * dilation // PLANES
        for res in range(dilation):
            planes = list(range(res, PLANES, dilation))
            blocks += [(pat, planes, per_plane, n) for n in range(blk // per_plane)]

    carried = {}

    def scores_stage(pat, planes, per_plane, n):
        lo = n * per_plane
        prev = keys_values(kp_ref, vp_ref, planes, blk - per_plane, per_plane) if n == 0 else carried["kv"]
        cur = carried["kv"] = keys_values(kc_ref, vc_ref, planes, lo, per_plane)
        keys, vals_a, vals_b = (jnp.concatenate([p, c], axis=0) for p, c in zip(prev, cur))
        return _block_scores(gather(q_ref, planes, lo, per_plane), keys, head_a), vals_a, vals_b

    def softmax_stage(pat, planes, per_plane, n, s, vals_a, vals_b):
        bias = bias_scr[has_prev, pat] if n == 0 else bias_scr[1, pat]
        return _block_softmax_pv(s, bias, vals_a, vals_b)

    def finish_stage(pat, planes, per_plane, n, accs, maxes):
        lo = n * per_plane
        out, den, mx = _block_finish(accs, maxes, head_a)
        for j, r in enumerate(planes):
            rows = slice(j * per_plane, (j + 1) * per_plane)
            o_scr[pat, r, lo:lo + per_plane, :] = out[rows]
            d_scr[pat, r, lo:lo + per_plane, :] = den[rows]
            m_scr[pat, r, lo:lo + per_plane, :] = mx[rows]

    in_scores, in_softmax = {}, {}
    for step in range(len(blocks) + ATTN_SKEW_SCORES + ATTN_SKEW_FINISH):
        if step < len(blocks):
            in_scores[step] = scores_stage(*blocks[step])
        mid = step - ATTN_SKEW_SCORES
        if 0 <= mid < len(blocks):
            in_softmax[mid] = softmax_stage(*blocks[mid], *in_scores.pop(mid))
        last = mid - ATTN_SKEW_FINISH
        if 0 <= last < len(blocks):
            finish_stage(*blocks[last], *in_softmax.pop(last))

    for r in range(PLANES):
        m = jnp.maximum(jnp.maximum(m_scr[0, r], m_scr[1, r]), m_scr[2, r])
        num = den = None
        for pat in range(len(dilations)):
            w = jnp.exp2(m_scr[pat, r] - m)
            num = w * o_scr[pat, r] if num is None else num + w * o_scr[pat, r]
            den = w * d_scr[pat, r] if den is None else den + w * d_scr[pat, r]
        y_scr[r % 4, pl.ds(r // 4, blk, stride=4), :] = num / den
    for a in range(4):
        y_ref[0, 0, pl.ds(a, PLANES * blk // 4, stride=4), :] = y_scr[a]


def _dilated_attention(q, k, v):
    batch, n_pair, planes, rows, lanes = q.shape
    blk = ATTN_BLOCK
    tile = planes * blk
    cur = pl.BlockSpec((1, 1, planes, blk, lanes), lambda b, h, t: (b, h, 0, t, 0))
    prev = pl.BlockSpec((1, 1, planes, blk, lanes), lambda b, h, t: (b, h, 0, jnp.maximum(t - 1, 0), 0))
    return pl.pallas_call(
        _attn_kernel,
        grid=(batch, n_pair, rows // blk),
        in_specs=[cur, prev, cur, prev, cur],
        out_specs=pl.BlockSpec((1, 1, tile, lanes), lambda b, h, t: (b, h, t, 0)),
        out_shape=jax.ShapeDtypeStruct((batch, n_pair, planes * rows, lanes), F32),
        scratch_shapes=[pltpu.VMEM((len(DILATED_PAIRS), planes, blk, lanes), F32)] * 3
        + [pltpu.VMEM((2, len(DILATED_PAIRS), blk, 2 * blk), F32),
           pltpu.VMEM((4, planes * blk // 4, lanes), F32)],
        compiler_params=pltpu.CompilerParams(
            dimension_semantics=("arbitrary",) * 3, vmem_limit_bytes=VMEM_LIMIT),
        name="dilated_attn",
    )(q, k, k, v, v)


def _hgrn_blocks(base, lb_all, nw_ref, hg_ref, y_ref, st_ref, b_scr, q_scr, k_scr, a_scr, tri, pair_key):
    t_blk = HGRN_BLOCK
    dk = HGRN_DK
    sub = HGRN_SUB
    w = HGRN_WIDTH
    rows = pl.ds(base, t_blk)
    heads = range(HGRN_HEADS)
    cols = [slice(h * dk, (h + 1) * dk) for h in heads]
    qf, key, v_bf, b, o = {}, {}, {}, {}, {}

    for h in heads:
        lb = lb_all[:, cols[h]]
        qr = hg_ref[rows, h * dk:(h + 1) * dk]
        fl = hg_ref[rows, w + h * dk:w + (h + 1) * dk]
        f = lb + (1.0 - lb) * _sigmoid(fl)
        lf = jnp.log2(f)
        key[h] = 1.0 - f
        qf[h] = qr * _sigmoid(qr)
        v_bf[h] = hg_ref[rows, 2 * w + h * dk:2 * w + (h + 1) * dk].astype(BF16)
        hi = lf.astype(BF16)
        r1 = lf - hi.astype(F32)
        mid = r1.astype(BF16)
        lo = (r1 - mid.astype(F32)).astype(BF16)
        b[h] = _dot(tri, hi) + _dot(tri, mid) + _dot(tri, lo)
        b_scr[h] = b[h]
        q_scr[h] = qf[h]
        k_scr[h] = jnp.log2(key[h]) - b[h]
        yield

    for h in heads:
        b_last = b_scr[h, t_blk - 1:t_blk, :]
        st = st_ref[h]
        q_inter = (qf[h] * jnp.exp2(b[h])).astype(BF16)
        o[h] = _dot_nt(q_inter, st.astype(BF16))
        k_upd = (key[h] * jnp.exp2(b_last - b[h])).astype(BF16)
        st_ref[h] = st * jnp.exp2(b_last) + _dot_tn(v_bf[h], k_upd)
    yield

    sub_id = lax.broadcasted_iota(jnp.int32, (sub, t_blk), 0)
    lane_id = lax.broadcasted_iota(jnp.int32, (sub, t_blk), 1)
    keep = [jnp.logical_and((lane_id & (sub - 1)) == s, sub_id >= s) for s in range(sub)]
    for c in range(t_blk // sub):
        lo_row = c * sub
        in_chunk = (lane_id & -sub) == lo_row
        for h in heads:
            b_t = b_scr[h, lo_row:lo_row + sub, :]
            q_t = q_scr[h, lo_row:lo_row + sub, :]
            blk_rows = jnp.zeros((sub, t_blk), F32)
            for s in range(sub):
                c_s = k_scr[h, lo_row + s:lo_row + s + 1, :]
                col = jnp.sum(q_t * jnp.exp2(b_t + c_s), axis=-1, keepdims=True)
                blk_rows = jnp.where(keep[s], col, blk_rows)
            a_scr[h, lo_row:lo_row + sub, :] = jnp.where(in_chunk, blk_rows, 0.0)
        if c % 2 == 1:
            yield

    a_mat = {h: a_scr[h] for h in heads}
    rid = lax.broadcasted_iota(jnp.int32, (t_blk, dk), 0)
    half = sub
    while half < t_blk:
        span = 2 * half
        upper = (rid & half) != 0
        for h in heads:
            refs = [jnp.broadcast_to(b_scr[h, p * span + half - 1:p * span + half, :], (span, dk))
                    for p in range(t_blk // span)]
            d = b[h] - (refs[0] if len(refs) == 1 else jnp.concatenate(refs, axis=0))
            x = (jnp.where(upper, qf[h], key[h]) * jnp.exp2(jnp.where(upper, d, -d))).astype(BF16)
            a_mat[h] = jnp.where(pair_key == half, _dot_nt(x, x), a_mat[h])
        half = span
        yield

    for h in heads:
        gr = hg_ref[rows, 3 * w + h * dk:3 * w + (h + 1) * dk]
        out = o[h] + _dot(a_mat[h].astype(BF16), v_bf[h])
        out = _rms_scale(out) * nw_ref[:, cols[h]]
        y_ref[rows, cols[h]] = (out * (gr * _sigmoid(gr))).astype(BF16)


def _in_proj_hgrn_kernel(x_ref, n1_ref, w_ref, *refs, tiles_per_seq):
    step = pl.program_id(0)
    last = pl.num_programs(0) - 1
    w_scr, st_ref = refs[9], refs[11]

    @pl.when(step == 0)
    def _():
        for col in range(0, IN_PROJ_WIDTH, IN_PROJ_CHUNK):
            w_scr[:, col:col + IN_PROJ_CHUNK] = w_ref[0, :, col:col + IN_PROJ_CHUNK].astype(BF16)
        _in_proj_hgrn_step(True, False, x_ref, n1_ref, *refs)

    @pl.when(step % tiles_per_seq == 1)
    def _():
        st_ref[...] = jnp.zeros_like(st_ref)

    @pl.when((step > 0) & (step < last))
    def _():
        _in_proj_hgrn_step(True, True, x_ref, n1_ref, *refs)

    @pl.when(step == last)
    def _():
        _in_proj_hgrn_step(False, True, x_ref, n1_ref, *refs)


def _in_proj_hgrn_step(do_proj, do_hgrn, x_ref, n1_ref, cos_ref, sin_ref, lbl_ref, nw_ref,
                       q_ref, k_ref, v_ref, y_ref,
                       hg_scr, w_scr, stage_ref, st_ref, b_scr, q_scr, k_scr, a_scr, split_ref):
    t_blk = HGRN_BLOCK
    step = pl.program_id(0)
    write_slot = step & 1

    u = (_rms_scale(x_ref[0]) * n1_ref[...]).astype(BF16)
    cos = cos_ref[...]
    sin = sin_ref[...]
    a = ATTN_WIDTH
    n_pair = a // LANES
    hg_out = hg_scr.at[write_slot]
    q_scale = HEAD_DIM ** -0.5 * float(np.log2(np.e))
    rows_per_plane = stage_ref.shape[1] // PLANES
    pieces = []

    def project(col, width):
        return _dot(u, w_scr[:, col:col + width])

    def attn_piece(i, j0, rotary, scale):
        def run():
            p = project(i * a + j0 * LANES, IN_PROJ_CHUNK)
            for jj in range(IN_PROJ_CHUNK // LANES):
                t = p[:, jj * LANES:(jj + 1) * LANES]
                if rotary:
                    t = _rotate_half_pairs(t, cos, sin)
                stage_ref[i * n_pair + j0 + jj] = t if scale is None else t * scale
        return run

    def plane_piece(i, out_ref, j):
        slab = i * n_pair + j
        quarter = stage_ref.shape[1] // 4

        def run():
            for a in range(4):
                split_ref[slab, a] = stage_ref[slab, pl.ds(a, quarter, stride=4), :]
            for r in range(PLANES):
                out_ref[0, j, r] = split_ref[slab, r % 4, pl.ds(r // 4, rows_per_plane, stride=4), :]
        return run

    def hg_piece(c):
        def run():
            hg_out[:, c * IN_PROJ_CHUNK:(c + 1) * IN_PROJ_CHUNK] = project(3 * a + c * IN_PROJ_CHUNK,
                                                                           IN_PROJ_CHUNK)
        return run

    hg_pieces = [hg_piece(c) for c in range(4 * HGRN_WIDTH // IN_PROJ_CHUNK)]
    for i, (out_ref, rotary, scale) in enumerate(((q_ref, True, q_scale), (k_ref, True, None),
                                                   (v_ref, False, None))):
        for j0 in range(0, n_pair, IN_PROJ_CHUNK // LANES):
            pieces.append(attn_piece(i, j0, rotary, scale))
            pieces.append(hg_pieces.pop(0))
        pieces += [plane_piece(i, out_ref, j) for j in range(n_pair)]
    pieces += hg_pieces
    if not do_proj:
        pieces = []
    if not do_hgrn:
        for run in pieces:
            run()
        return

    lbl = lbl_ref[...]
    mx = jnp.max(lbl, axis=0, keepdims=True)
    e = jnp.exp(lbl - mx)
    lb_all = e[0:1, :] / jnp.sum(e, axis=0, keepdims=True)

    row = lax.broadcasted_iota(jnp.int32, (t_blk, t_blk), 0)
    col = lax.broadcasted_iota(jnp.int32, (t_blk, t_blk), 1)
    tri = (col <= row).astype(BF16)
    pair_key = jnp.zeros((t_blk, t_blk), jnp.int32)
    half = HGRN_SUB
    while half < t_blk:
        pair_key = jnp.where(jnp.logical_and((row ^ col) >= half, col < row), half, pair_key)
        half *= 2

    hg_in = hg_scr.at[1 - write_slot]
    y_out = y_ref.at[0]
    n_blocks = hg_scr.shape[1] // t_blk
    stages = [_hgrn_blocks(i * t_blk, lb_all, nw_ref, hg_in, y_out, st_ref, b_scr, q_scr, k_scr, a_scr,
                           tri, pair_key) for i in range(n_blocks)]
    for gen in stages:
        for _ in gen:
            if pieces:
                pieces.pop(0)()
    assert not pieces


def _in_proj_hgrn(x, norm1_w, w_in, cos_t, sin_t, lb_logits, hgrn_norm_w):
    batch, seq, _ = x.shape
    tm = ROW_TILE
    n_pair = ATTN_WIDTH // LANES
    tiles_per_seq = seq // tm
    n_tiles = batch * tiles_per_seq

    def proj_tile(s):
        s = jnp.minimum(s, n_tiles - 1)
        return s // tiles_per_seq, s % tiles_per_seq

    def hgrn_tile(s):
        s = jnp.maximum(s - 1, 0)
        return s // tiles_per_seq, s % tiles_per_seq

    plane_spec = pl.BlockSpec((1, n_pair, PLANES, tm // PLANES, LANES),
                              lambda s: (proj_tile(s)[0], 0, 0, proj_tile(s)[1], 0))
    plane_shape = jax.ShapeDtypeStruct((batch, n_pair, PLANES, seq // PLANES, LANES), F32)
    tab_spec = pl.BlockSpec((tm, LANES), lambda s: (proj_tile(s)[1], 0))
    blk_scratch = pltpu.VMEM((HGRN_HEADS, HGRN_BLOCK, HGRN_DK), F32)
    return pl.pallas_call(
        functools.partial(_in_proj_hgrn_kernel, tiles_per_seq=tiles_per_seq),
        grid=(n_tiles + 1,),
        in_specs=[pl.BlockSpec((1, tm, D_MODEL), lambda s: (*proj_tile(s), 0)), _resident((1, D_MODEL)),
                  _resident(w_in.shape), tab_spec, tab_spec,
                  _resident(lb_logits.shape), _resident((1, HGRN_WIDTH))],
        out_specs=[plane_spec] * 3 + [pl.BlockSpec((1, tm, HGRN_WIDTH), lambda s: (*hgrn_tile(s), 0))],
        out_shape=[plane_shape] * 3 + [jax.ShapeDtypeStruct((batch, seq, HGRN_WIDTH), BF16)],
        scratch_shapes=[pltpu.VMEM((2, tm, 4 * HGRN_WIDTH), F32),
                        pltpu.VMEM((D_MODEL, IN_PROJ_WIDTH), BF16),
                        pltpu.VMEM((3 * n_pair, tm, LANES), F32),
                        pltpu.VMEM((HGRN_HEADS, HGRN_DK, HGRN_DK), F32),
                        blk_scratch, blk_scratch, blk_scratch,
                        pltpu.VMEM((HGRN_HEADS, HGRN_BLOCK, HGRN_BLOCK), F32),
                        pltpu.VMEM((3 * n_pair, 4, tm // 4, LANES), F32)],
        compiler_params=pltpu.CompilerParams(
            dimension_semantics=("arbitrary",), vmem_limit_bytes=VMEM_LIMIT),
        name="in_proj_hgrn2",
    )(x, norm1_w, w_in, cos_t, sin_t, lb_logits, hgrn_norm_w)


def _out_ffn_kernel(ya_ref, yb_ref, x_ref, wo_ref, n2_ref, wgu_ref, wd_ref, nf_ref, out_ref):
    chunk = FFN_CHUNK
    mixed = jnp.concatenate([ya_ref[0, j].astype(BF16) for j in range(ATTN_WIDTH // LANES)]
                            + [yb_ref[0]], axis=1)
    h = x_ref[0] + _dot(mixed, wo_ref[0].astype(BF16))
    u = (_rms_scale(h) * n2_ref[...]).astype(BF16)
    ffn = jnp.zeros_like(h)
    for c in range(FFN_HIDDEN // chunk):
        cols = slice(c * chunk, (c + 1) * chunk)
        up_cols = slice(FFN_HIDDEN + c * chunk, FFN_HIDDEN + (c + 1) * chunk)
        gate = _dot(u, wgu_ref[0, :, cols].astype(BF16))
        up = _dot(u, wgu_ref[0, :, up_cols].astype(BF16))
        act = (gate * _sigmoid(gate) * up).astype(BF16)
        ffn = ffn + _dot(act, wd_ref[0, cols, :].astype(BF16))
    out_ref[0] = _rms_scale(h + ffn) * nf_ref[...]


def _out_ffn(ya, yb, x, w_out, norm2_w, w_gate_up, w_down, final_w):
    batch, seq, _ = x.shape
    tm = ROW_TILE
    n_pair = ATTN_WIDTH // LANES
    row_spec = lambda width: pl.BlockSpec((1, tm, width), lambda b, i: (b, i, 0))
    return pl.pallas_call(
        _out_ffn_kernel,
        grid=(batch, seq // tm),
        in_specs=[pl.BlockSpec((1, n_pair, tm, LANES), lambda b, i: (b, 0, i, 0)),
                  row_spec(HGRN_WIDTH), row_spec(D_MODEL),
                  _resident(w_out.shape), _resident((1, D_MODEL)), _resident(w_gate_up.shape),
                  _resident(w_down.shape), _resident((1, D_MODEL))],
        out_specs=row_spec(D_MODEL),
        out_shape=jax.ShapeDtypeStruct((batch, seq, D_MODEL), F32),
        compiler_params=pltpu.CompilerParams(
            dimension_semantics=("arbitrary", "arbitrary"), vmem_limit_bytes=VMEM_LIMIT),
        name="out_proj_ffn",
    )(ya, yb, x, w_out, norm2_w, w_gate_up, w_down, final_w)


def kernel(x, norm1_w, w_in, lb_logits, hgrn_norm_w, w_out, norm2_w, w_gate_up, w_down, final_norm_w):
    batch, seq, d_model = x.shape
    assert d_model == D_MODEL and norm1_w.shape[0] == 1
    assert all(w // d == ATTN_BLOCK for w, d in DILATED_PAIRS)
    assert tuple(d for _, d in DILATED_PAIRS) == (1, 4, 16) and seq % (PLANES * ATTN_BLOCK) == 0
    cos_t, sin_t = _rope_tables(seq)
    q, k, v, yb = _in_proj_hgrn(x, norm1_w, w_in, cos_t, sin_t, lb_logits, hgrn_norm_w)
    ya = _dilated_attention(q, k, v)
    return _out_ffn(ya, yb, x, w_out, norm2_w, w_gate_up, w_down, final_norm_w.reshape(1, d_model))
```
